```python
import jax, jax.numpy as jnp
from jax import lax
import numpy as np

D_MODEL = 1024
BATCH = 4
SEQ = 8192
DEPTH = 1

CHUNK = 64

A_HEADS = 8
A_HEAD_DIM = 64
A_WIDTH = A_HEADS * A_HEAD_DIM
A_LEFT_CHUNKS = 8
A_BAND = (A_LEFT_CHUNKS + 1) * CHUNK
REL_CLIP = 128

B_HEADS = 8
B_NOPE_DIM = 64
B_ROPE_DIM = 32
B_QK_DIM = B_NOPE_DIM + B_ROPE_DIM
B_V_DIM = 64
B_WIDTH = B_HEADS * B_V_DIM
Q_LORA = 256
KV_LORA = 128
ROPE_THETA = 10000.0
Q_BLOCK = 128

DEEPNORM_ALPHA = (2 * DEPTH) ** 0.25
DEEPNORM_BETA = (8 * DEPTH) ** -0.25
LN_EPS = 1e-5
RMS_EPS = 1e-6
NEG_INF = -1e30

IN_SPLITS = (A_WIDTH, A_WIDTH, A_WIDTH, A_WIDTH,
             Q_LORA, KV_LORA, B_ROPE_DIM, B_WIDTH,
             D_MODEL, D_MODEL)
IN_COLS = sum(IN_SPLITS)

kernel_name = "hybrid_chunk_relpos_mla_deepnorm"


def _split_points():
    pts, acc = [], 0
    for w in IN_SPLITS[:-1]:
        acc += w
        pts.append(acc)
    return pts


def layer_norm(x, g, b):
    xf = x.astype(jnp.float32)
    mu = jnp.mean(xf, axis=-1, keepdims=True)
    var = jnp.mean(jnp.square(xf - mu), axis=-1, keepdims=True)
    return ((xf - mu) * lax.rsqrt(var + LN_EPS) * g.astype(jnp.float32) + b.astype(jnp.float32)).astype(x.dtype)


def rms_norm(x, g):
    xf = x.astype(jnp.float32)
    return (xf * lax.rsqrt(jnp.mean(jnp.square(xf), axis=-1, keepdims=True) + RMS_EPS) * g.astype(jnp.float32)).astype(x.dtype)


def rope(x, positions):
    half = x.shape[-1] // 2
    inv_freq = ROPE_THETA ** (-jnp.arange(half, dtype=jnp.float32) / half)
    ang = positions.astype(jnp.float32)[..., None] * inv_freq
    cos = jnp.cos(ang)[:, :, None, :]
    sin = jnp.sin(ang)[:, :, None, :]
    x1 = x[..., :half].astype(jnp.float32)
    x2 = x[..., half:].astype(jnp.float32)
    return jnp.concatenate([x1 * cos - x2 * sin, x2 * cos + x1 * sin], axis=-1).astype(x.dtype)


def chunked_relpos_attention(q, k, v, rel_bias):
    B, S, H, Dh = q.shape
    n_chunks = S // CHUNK
    pad = A_LEFT_CHUNKS * CHUNK
    k_pad = jnp.pad(k, ((0, 0), (pad, 0), (0, 0), (0, 0)))
    v_pad = jnp.pad(v, ((0, 0), (pad, 0), (0, 0), (0, 0)))
    rel = jnp.clip(jnp.arange(CHUNK)[:, None] + pad - jnp.arange(A_BAND)[None, :], -REL_CLIP, REL_CLIP) + REL_CLIP
    bias = jnp.transpose(rel_bias[rel], (2, 0, 1)).astype(jnp.float32)
    scale = Dh ** -0.5

    def one_chunk(c):
        start = c * CHUNK
        qc = lax.dynamic_slice_in_dim(q, start, CHUNK, axis=1)
        kc = lax.dynamic_slice_in_dim(k_pad, start, A_BAND, axis=1)
        vc = lax.dynamic_slice_in_dim(v_pad, start, A_BAND, axis=1)
        s = jnp.einsum('bqhd,bkhd->bhqk', qc, kc).astype(jnp.float32) * scale + bias
        valid = (start - pad + jnp.arange(A_BAND)) >= 0
        s = jnp.where(valid[None, None, None, :], s, NEG_INF)
        p = jax.nn.softmax(s, axis=-1).astype(vc.dtype)
        return jnp.einsum('bhqk,bkhd->bqhd', p, vc)

    out = lax.map(one_chunk, jnp.arange(n_chunks))
    return jnp.transpose(out, (1, 0, 2, 3, 4)).reshape(B, S, H, Dh)


def chunk_causal_attention(q, k, v):
    B, S, H, Dqk = q.shape
    n_blocks = S // Q_BLOCK
    scale = Dqk ** -0.5
    key_chunk = jnp.arange(S) // CHUNK

    def one_block(i):
        start = i * Q_BLOCK
        qb = lax.dynamic_slice_in_dim(q, start, Q_BLOCK, axis=1)
        s = jnp.einsum('bqhd,bkhd->bhqk', qb, k).astype(jnp.float32) * scale
        q_chunk = (start + jnp.arange(Q_BLOCK)) // CHUNK
        mask = key_chunk[None, :] <= q_chunk[:, None]
        s = jnp.where(mask[None, None], s, NEG_INF)
        p = jax.nn.softmax(s, axis=-1).astype(v.dtype)
        return jnp.einsum('bhqk,bkhd->bqhd', p, v)

    out = lax.map(one_block, jnp.arange(n_blocks))
    return jnp.transpose(out, (1, 0, 2, 3, 4)).reshape(B, S, H, v.shape[-1])


def setup_inputs(seed: int = 0) -> dict:
    key = jax.random.key(seed)
    ks = jax.random.split(key, 16)
    f32 = jnp.float32
    nrm = lambda k, shape, s: jax.random.normal(k, shape, f32) * s
    return {
        "x": jax.random.normal(ks[0], (BATCH, SEQ, D_MODEL), f32),
        "positions": jnp.broadcast_to(jnp.arange(SEQ, dtype=jnp.int32), (BATCH, SEQ)),
        "ln_in_g": 1.0 + nrm(ks[1], (D_MODEL,), 0.01),
        "ln_in_b": nrm(ks[2], (D_MODEL,), 0.01),
        "w_in": nrm(ks[3], (DEPTH, D_MODEL, IN_COLS), D_MODEL ** -0.5),
        "b_in": nrm(ks[4], (DEPTH, IN_COLS), 0.01),
        "q_norm_g": 1.0 + nrm(ks[5], (DEPTH, Q_LORA), 0.01),
        "kv_norm_g": 1.0 + nrm(ks[6], (DEPTH, KV_LORA), 0.01),
        "w_uq": nrm(ks[7], (DEPTH, Q_LORA, B_HEADS * B_QK_DIM), Q_LORA ** -0.5),
        "w_ukv": nrm(ks[8], (DEPTH, KV_LORA, B_HEADS * (B_NOPE_DIM + B_V_DIM)), KV_LORA ** -0.5),
        "rel_bias": nrm(ks[9], (DEPTH, 2 * REL_CLIP + 1, A_HEADS), 0.2),
        "w_proj_a": nrm(ks[10], (DEPTH, A_WIDTH, D_MODEL), A_WIDTH ** -0.5 * DEEPNORM_BETA),
        "w_proj_b": nrm(ks[11], (DEPTH, B_WIDTH, D_MODEL), B_WIDTH ** -0.5 * DEEPNORM_BETA),
        "w_out": nrm(ks[12], (DEPTH, D_MODEL, D_MODEL), D_MODEL ** -0.5 * DEEPNORM_BETA),
        "ln_post_g": 1.0 + nrm(ks[13], (DEPTH, D_MODEL), 0.01),
        "ln_post_b": nrm(ks[14], (DEPTH, D_MODEL), 0.01),
    }


def reference(x, positions, ln_in_g, ln_in_b, w_in, b_in, q_norm_g, kv_norm_g, w_uq, w_ukv,
              rel_bias, w_proj_a, w_proj_b, w_out, ln_post_g, ln_post_b):
    B, S, _ = x.shape
    h = layer_norm(x, ln_in_g, ln_in_b)
    pts = _split_points()
    for l in range(DEPTH):
        proj = h @ w_in[l] + b_in[l]
        a_q, a_k, a_v, a_z, b_cq, b_ckv, b_kr, b_z, g_a, g_b = jnp.split(proj, pts, axis=-1)

        ya = chunked_relpos_attention(a_q.reshape(B, S, A_HEADS, A_HEAD_DIM),
                                      a_k.reshape(B, S, A_HEADS, A_HEAD_DIM),
                                      a_v.reshape(B, S, A_HEADS, A_HEAD_DIM), rel_bias[l])
        ya = (ya.reshape(B, S, A_WIDTH) * jax.nn.silu(a_z)) @ w_proj_a[l]

        cq = rms_norm(b_cq, q_norm_g[l])
        qb = (cq @ w_uq[l]).reshape(B, S, B_HEADS, B_QK_DIM)
        q_full = jnp.concatenate([qb[..., :B_NOPE_DIM], rope(qb[..., B_NOPE_DIM:], positions)], axis=-1)
        ckv = rms_norm(b_ckv, kv_norm_g[l])
        kv = (ckv @ w_ukv[l]).reshape(B, S, B_HEADS, B_NOPE_DIM + B_V_DIM)
        k_pe = rope(b_kr[:, :, None, :], positions)
        k_full = jnp.concatenate([kv[..., :B_NOPE_DIM],
                                  jnp.broadcast_to(k_pe, (B, S, B_HEADS, B_ROPE_DIM))], axis=-1)
        yb = chunk_causal_attention(q_full, k_full, kv[..., B_NOPE_DIM:])
        yb = (yb.reshape(B, S, B_WIDTH) * jax.nn.silu(b_z)) @ w_proj_b[l]

        mixed = jax.nn.sigmoid(g_a) * ya + jax.nn.sigmoid(g_b) * yb
        out = mixed @ w_out[l]
        h = layer_norm(DEEPNORM_ALPHA * h + out, ln_post_g[l], ln_post_b[l])
    return h
```

```python
import functools

import jax
import jax.numpy as jnp
from jax import lax
from jax.experimental import pallas as pl
from jax.experimental.pallas import tpu as pltpu

D_MODEL = 1024
CHUNK = 64
A_HEADS = 8
A_HEAD_DIM = 64
A_WIDTH = A_HEADS * A_HEAD_DIM
A_LEFT_CHUNKS = 8
REL_CLIP = 128
B_HEADS = 8
B_NOPE_DIM = 64
B_ROPE_DIM = 32
B_QK_DIM = B_NOPE_DIM + B_ROPE_DIM
B_V_DIM = 64
B_WIDTH = B_HEADS * B_V_DIM
Q_LORA = 256
KV_LORA = 128
ROPE_THETA = 10000.0
DEEPNORM_ALPHA = 2.0 ** 0.25
LN_EPS = 1e-5
RMS_EPS = 1e-6
NEG_INF = -1e30

LANES = 128
VMEM_LIMIT = 56 * 1024 * 1024

PROJ_TM = 512
A_TQ = 256
A_WIN = A_TQ + A_LEFT_CHUNKS * CHUNK
B_TQ = 512
B_TK = 512
OUT_TM = 256

BF16 = jnp.bfloat16
F32 = jnp.float32


def _layer_norm(x, g, b):
    mu = jnp.mean(x, axis=-1, keepdims=True)
    xc = x - mu
    var = jnp.mean(xc * xc, axis=-1, keepdims=True)
    return xc * lax.rsqrt(var + LN_EPS) * g + b


def _rms_norm(x, g):
    return x * lax.rsqrt(jnp.mean(x * x, axis=-1, keepdims=True) + RMS_EPS) * g


def _const_spec(shape):
    nd = len(shape)
    return pl.BlockSpec(shape, lambda *_: (0,) * nd, pipeline_mode=pl.Buffered(1))


def _proj_kernel(x_ref, pos_ref, lng_ref, lnb_ref, w1_ref, b1_ref, qg_ref, kvg_ref, wq_ref, wk_ref, wvt_ref,
                 freq_ref, aq_ref, ak_ref, av_ref, qb_ref, kb_ref, vbt_ref):
    h = _layer_norm(x_ref[0], lng_ref[...], lnb_ref[...])
    proj = jnp.dot(h.astype(BF16), w1_ref[...], preferred_element_type=F32) + b1_ref[...]
    aq_ref[0] = proj[:, 0:A_WIDTH].astype(BF16)
    ak_ref[0] = proj[:, A_WIDTH:2 * A_WIDTH].astype(BF16)
    av_ref[0] = proj[:, 2 * A_WIDTH:3 * A_WIDTH].astype(BF16)
    o = 3 * A_WIDTH
    cq = _rms_norm(proj[:, o:o + Q_LORA], qg_ref[...]).astype(BF16)
    o += Q_LORA
    ckv = _rms_norm(proj[:, o:o + KV_LORA], kvg_ref[...]).astype(BF16)
    o += KV_LORA
    kr = proj[:, o:o + LANES]
    kr_rot = proj[:, o + LANES:o + 2 * LANES]

    ang = pos_ref[0].astype(F32) * freq_ref[...]
    cos_t = jnp.cos(ang)
    sin_t = jnp.sin(ang)
    lane = lax.broadcasted_iota(jnp.int32, (1, LANES), 1)
    scale = B_QK_DIM ** -0.5
    tq = jnp.where(lane < B_NOPE_DIM, scale, jnp.where(lane < B_QK_DIM, cos_t, sin_t) * scale)
    k_rope = kr * cos_t + kr_rot * sin_t

    q_all = jnp.dot(cq, wq_ref[...], preferred_element_type=F32)
    k_all = jnp.dot(ckv, wk_ref[...], preferred_element_type=F32)
    for hd in range(B_HEADS):
        sl = slice(hd * LANES, (hd + 1) * LANES)
        qb_ref[0, :, sl] = (q_all[:, sl] * tq).astype(BF16)
        kb_ref[0, :, sl] = (k_all[:, sl] + k_rope).astype(BF16)
    vbt_ref[0] = lax.dot_general(wvt_ref[...], ckv, (((1,), (1,)), ((), ())),
                                 preferred_element_type=F32).astype(BF16)


def _token_projections(x, pos3, lng, lnb, w1, b1, qg, kvg, wq, wk, wvt, freq):
    B, S, _ = x.shape
    tm = PROJ_TM
    n1 = w1.shape[1]
    row = lambda w: pl.BlockSpec((1, tm, w), lambda b, i: (b, i, 0))
    return pl.pallas_call(
        _proj_kernel,
        grid=(B, S // tm),
        in_specs=[row(D_MODEL), row(1),
                  _const_spec((1, D_MODEL)), _const_spec((1, D_MODEL)),
                  _const_spec((D_MODEL, n1)), _const_spec((1, n1)),
                  _const_spec((1, Q_LORA)), _const_spec((1, KV_LORA)),
                  _const_spec((Q_LORA, B_HEADS * LANES)), _const_spec((KV_LORA, B_HEADS * LANES)),
                  _const_spec((B_WIDTH, KV_LORA)), _const_spec((1, LANES))],
        out_specs=[row(A_WIDTH), row(A_WIDTH), row(A_WIDTH), row(B_HEADS * LANES), row(B_HEADS * LANES),
                   pl.BlockSpec((1, B_WIDTH, tm), lambda b, i: (b, 0, i))],
        out_shape=[jax.ShapeDtypeStruct((B, S, A_WIDTH), BF16)] * 3
                  + [jax.ShapeDtypeStruct((B, S, B_HEADS * LANES), BF16)] * 2
                  + [jax.ShapeDtypeStruct((B, B_WIDTH, S), BF16)],
        compiler_params=pltpu.CompilerParams(dimension_semantics=("arbitrary", "arbitrary"),
                                             vmem_limit_bytes=VMEM_LIMIT),
        name="token_projections",
    )(x, pos3, lng, lnb, w1, b1, qg, kvg, wq, wk, wvt, freq)


def _mixer_a_kernel(q_ref, k2_ref, k1_ref, k0_ref, v2_ref, v1_ref, v0_ref, bias_ref, o_ref):
    i = pl.program_id(1)
    q = q_ref[0]
    k = jnp.concatenate([k2_ref[0], k1_ref[0], k0_ref[0]], axis=0)
    v = jnp.concatenate([v2_ref[0], v1_ref[0], v0_ref[0]], axis=0)
    kpos = lax.broadcasted_iota(jnp.int32, (1, A_WIN), 1) + (i * A_TQ - A_LEFT_CHUNKS * CHUNK)
    pad_mask = jnp.where(kpos >= 0, 0.0, NEG_INF).astype(F32)
    lane = lax.broadcasted_iota(jnp.int32, (1, LANES), 1)
    for pair in range(A_HEADS // 2):
        sl = slice(pair * LANES, (pair + 1) * LANES)
        qp, kp, vp = q[:, sl], k[:, sl], v[:, sl]
        out = None
        for hh in range(2):
            head_lanes = (lane // A_HEAD_DIM) == hh
            qm = jnp.where(head_lanes, qp, jnp.zeros_like(qp))
            vm = jnp.where(head_lanes, vp, jnp.zeros_like(vp))
            s = lax.dot_general(qm, kp, (((1,), (1,)), ((), ())), preferred_element_type=F32)
            s = s + bias_ref[2 * pair + hh] + pad_mask
            mx = jnp.max(s, axis=-1, keepdims=True)
            e = jnp.exp(s - mx)
            l = jnp.sum(e, axis=-1, keepdims=True)
            o = jnp.dot(e.astype(BF16), vm, preferred_element_type=F32) * (1.0 / l)
            out = o if out is None else out + o
        o_ref[0, :, sl] = out.astype(BF16)


def _mixer_a(aq, ak, av, bias):
    B, S, W = aq.shape
    tq = A_TQ
    blk = lambda back: pl.BlockSpec((1, tq, W), lambda b, i: (b, jnp.maximum(i - back, 0), 0))
    return pl.pallas_call(
        _mixer_a_kernel,
        grid=(B, S // tq),
        in_specs=[blk(0), blk(2), blk(1), blk(0), blk(2), blk(1), blk(0), _const_spec(bias.shape)],
        out_specs=blk(0),
        out_shape=jax.ShapeDtypeStruct((B, S, W), BF16),
        compiler_params=pltpu.CompilerParams(dimension_semantics=("arbitrary", "arbitrary"),
                                             vmem_limit_bytes=VMEM_LIMIT),
        name="mixer_a",
    )(aq, ak, ak, ak, av, av, av, bias)


def _mixer_a_bias(rel_bias):
    qi = jnp.arange(A_TQ)[:, None]
    kj = jnp.arange(A_WIN)[None, :]
    dist = qi + A_LEFT_CHUNKS * CHUNK - kj
    rel = jnp.clip(dist, -REL_CLIP, REL_CLIP) + REL_CLIP
    chunk_gap = qi // CHUNK + A_LEFT_CHUNKS - kj // CHUNK
    in_band = (chunk_gap >= 0) & (chunk_gap <= A_LEFT_CHUNKS)
    table = jnp.transpose(rel_bias[rel], (2, 0, 1)).astype(F32)
    return jnp.where(in_band[None], table, NEG_INF)


def _mixer_b_kernel(q_ref, k_ref, vt_ref, o_ref, m_ref, l_ref, acc_ref):
    qi = pl.program_id(2)
    q = q_ref[0]
    m_ref[...] = jnp.full(m_ref.shape, NEG_INF, F32)
    l_ref[...] = jnp.zeros(l_ref.shape, F32)
    acc_ref[...] = jnp.zeros(acc_ref.shape, F32)

    def step(j, masked):
        start = pl.multiple_of(j * B_TK, B_TK)
        k = k_ref[0, pl.ds(start, B_TK), :]
        s = lax.dot_general(k, q, (((1,), (1,)), ((), ())), preferred_element_type=F32)
        if masked:
            kc = lax.broadcasted_iota(jnp.int32, (B_TK, B_TQ), 0) // CHUNK
            qc = lax.broadcasted_iota(jnp.int32, (B_TK, B_TQ), 1) // CHUNK
            s = jnp.where(kc <= qc, s, NEG_INF)
        m_old = m_ref[...]
        m_new = jnp.maximum(m_old, jnp.max(s, axis=0, keepdims=True))
        p = jnp.exp(s - m_new)
        alpha = jnp.exp(m_old - m_new)
        l_ref[...] = alpha * l_ref[...] + jnp.sum(p, axis=0, keepdims=True)
        vt = vt_ref[0, :, pl.ds(start, B_TK)]
        acc_ref[...] = alpha * acc_ref[...] + jnp.dot(vt, p.astype(BF16), preferred_element_type=F32)
        m_ref[...] = m_new

    def body(j, carry):
        step(j, masked=False)
        return carry

    lax.fori_loop(0, qi, body, 0)
    step(qi, masked=True)
    o_ref[0] = (acc_ref[...] * (1.0 / l_ref[...])).astype(BF16)


def _mixer_b(qb, kb, vbt):
    B, S, _ = qb.shape
    return pl.pallas_call(
        _mixer_b_kernel,
        grid=(B, B_HEADS, S // B_TQ),
        in_specs=[pl.BlockSpec((1, B_TQ, LANES), lambda b, h, i: (b, i, h)),
                  pl.BlockSpec((1, S, LANES), lambda b, h, i: (b, 0, h)),
                  pl.BlockSpec((1, B_V_DIM, S), lambda b, h, i: (b, h, 0))],
        out_specs=pl.BlockSpec((1, B_V_DIM, B_TQ), lambda b, h, i: (b, h, i)),
        out_shape=jax.ShapeDtypeStruct((B, B_WIDTH, S), BF16),
        scratch_shapes=[pltpu.VMEM((1, B_TQ), F32), pltpu.VMEM((1, B_TQ), F32), pltpu.VMEM((B_V_DIM, B_TQ), F32)],
        compiler_params=pltpu.CompilerParams(dimension_semantics=("arbitrary", "arbitrary", "arbitrary"),
                                             vmem_limit_bytes=VMEM_LIMIT),
        name="mixer_b",
    )(qb, kb, vbt)


def _merge_kernel(x_ref, ya_ref, ybt_ref, lng_ref, lnb_ref, wzg_ref, bzg_ref, wpa_ref, wpb_ref, wout_ref,
                  pg_ref, pb_ref, o_ref):
    h = _layer_norm(x_ref[0], lng_ref[...], lnb_ref[...])
    zg = jnp.dot(h.astype(BF16), wzg_ref[...], preferred_element_type=F32) + bzg_ref[...]
    za = zg[:, 0:A_WIDTH]
    zb = zg[:, A_WIDTH:A_WIDTH + B_WIDTH]
    ga = zg[:, A_WIDTH + B_WIDTH:A_WIDTH + B_WIDTH + D_MODEL]
    gb = zg[:, A_WIDTH + B_WIDTH + D_MODEL:]
    ya_in = ya_ref[0].astype(F32) * (za * jax.nn.sigmoid(za))
    yb_in = ybt_ref[0].astype(F32).T * (zb * jax.nn.sigmoid(zb))
    ya = jnp.dot(ya_in.astype(BF16), wpa_ref[...], preferred_element_type=F32)
    yb = jnp.dot(yb_in.astype(BF16), wpb_ref[...], preferred_element_type=F32)
    mixed = jax.nn.sigmoid(ga) * ya + jax.nn.sigmoid(gb) * yb
    out = jnp.dot(mixed.astype(BF16), wout_ref[...], preferred_element_type=F32)
    o_ref[0] = _layer_norm(DEEPNORM_ALPHA * h + out, pg_ref[...], pb_ref[...])


def _merge(x, ya, ybt, lng, lnb, wzg, bzg, wpa, wpb, wout, pg, pb):
    B, S, _ = x.shape
    tm = OUT_TM
    row = lambda w: pl.BlockSpec((1, tm, w), lambda b, i: (b, i, 0))
    nzg = wzg.shape[1]
    return pl.pallas_call(
        _merge_kernel,
        grid=(B, S // tm),
        in_specs=[row(D_MODEL), row(A_WIDTH), pl.BlockSpec((1, B_WIDTH, tm), lambda b, i: (b, 0, i)),
                  _const_spec((1, D_MODEL)), _const_spec((1, D_MODEL)),
                  _const_spec((D_MODEL, nzg)), _const_spec((1, nzg)),
                  _const_spec((A_WIDTH, D_MODEL)), _const_spec((B_WIDTH, D_MODEL)),
                  _const_spec((D_MODEL, D_MODEL)), _const_spec((1, D_MODEL)), _const_spec((1, D_MODEL))],
        out_specs=row(D_MODEL),
        out_shape=jax.ShapeDtypeStruct((B, S, D_MODEL), F32),
        compiler_params=pltpu.CompilerParams(dimension_semantics=("arbitrary", "arbitrary"),
                                             vmem_limit_bytes=VMEM_LIMIT),
        name="merge_out",
    )(x, ya, ybt, lng, lnb, wzg, bzg, wpa, wpb, wout, pg, pb)


def _rot_cols(w):
    half = w.shape[-1] // 2
    return jnp.concatenate([-w[..., half:], w[..., :half]], axis=-1)


def _prep_layer(w_in, b_in, w_uq, w_ukv):
    c = 0
    cols = {}
    for name, width in (("aq", A_WIDTH), ("ak", A_WIDTH), ("av", A_WIDTH), ("az", A_WIDTH), ("cq", Q_LORA),
                        ("ckv", KV_LORA), ("kr", B_ROPE_DIM), ("bz", B_WIDTH), ("ga", D_MODEL), ("gb", D_MODEL)):
        cols[name] = slice(c, c + width)
        c += width
    w = lambda n: w_in[:, cols[n]]
    b = lambda n: b_in[cols[n]]
    a_scale = A_HEAD_DIM ** -0.5
    zeros_w = jnp.zeros((D_MODEL, B_NOPE_DIM), F32)
    zeros_b = jnp.zeros((B_NOPE_DIM,), F32)
    w1 = jnp.concatenate([w("aq") * a_scale, w("ak"), w("av"), w("cq"), w("ckv"),
                          zeros_w, w("kr"), w("kr"),
                          zeros_w, _rot_cols(w("kr")), _rot_cols(w("kr"))], axis=1)
    b1 = jnp.concatenate([b("aq") * a_scale, b("ak"), b("av"), b("cq"), b("ckv"),
                          zeros_b, b("kr"), b("kr"),
                          zeros_b, _rot_cols(b("kr")), _rot_cols(b("kr"))])
    wzg = jnp.concatenate([w("az"), w("bz"), w("ga"), w("gb")], axis=1)
    bzg = jnp.concatenate([b("az"), b("bz"), b("ga"), b("gb")])

    uq = w_uq.reshape(Q_LORA, B_HEADS, B_QK_DIM)
    uq_rope = uq[:, :, B_NOPE_DIM:]
    wq = jnp.concatenate([uq, _rot_cols(uq_rope)], axis=-1).reshape(Q_LORA, B_HEADS * LANES)
    ukv = w_ukv.reshape(KV_LORA, B_HEADS, B_NOPE_DIM + B_V_DIM)
    wk = jnp.concatenate([ukv[:, :, :B_NOPE_DIM], jnp.zeros((KV_LORA, B_HEADS, LANES - B_NOPE_DIM), F32)],
                         axis=-1).reshape(KV_LORA, B_HEADS * LANES)
    wvt = ukv[:, :, B_NOPE_DIM:].reshape(KV_LORA, B_WIDTH).T
    return (w1.astype(BF16), b1[None, :], wq.astype(BF16), wk.astype(BF16), wvt.astype(BF16),
            wzg.astype(BF16), bzg[None, :])


def _rope_freq_row():
    half = B_ROPE_DIM // 2
    inv_freq = ROPE_THETA ** (-jnp.arange(half, dtype=F32) / half)
    return jnp.concatenate([jnp.zeros((B_NOPE_DIM,), F32), inv_freq, inv_freq, inv_freq, inv_freq])[None, :]


def kernel(x, positions, ln_in_g, ln_in_b, w_in, b_in, q_norm_g, kv_norm_g, w_uq, w_ukv, rel_bias, w_proj_a,
           w_proj_b, w_out, ln_post_g, ln_post_b):
    depth = w_in.shape[0]
    assert depth == 1, "the trunk-entry norm is recomputed per kernel, which is only valid for one layer"
    B, S, _ = x.shape
    pos3 = positions.reshape(B, S, 1)
    lng, lnb = ln_in_g[None, :], ln_in_b[None, :]
    freq = _rope_freq_row()
    l = 0
    w1, b1, wq, wk, wvt, wzg, bzg = _prep_layer(w_in[l], b_in[l], w_uq[l], w_ukv[l])
    aq, ak, av, qb, kb, vbt = _token_projections(x, pos3, lng, lnb, w1, b1, q_norm_g[l][None, :],
                                                 kv_norm_g[l][None, :], wq, wk, wvt, freq)
    ya = _mixer_a(aq, ak, av, _mixer_a_bias(rel_bias[l]))
    ybt = _mixer_b(qb, kb, vbt)
    return _merge(x, ya, ybt, lng, lnb, wzg, bzg, w_proj_a[l].astype(BF16), w_proj_b[l].astype(BF16),
                  w_out[l].astype(BF16), ln_post_g[l][None, :], ln_post_b[l][None, :])
```

```python
import functools

import jax
import jax.numpy as jnp
from jax import lax
from jax.experimental import pallas as pl
from jax.experimental.pallas import tpu as pltpu

D_MODEL = 1024
CHUNK = 64
A_HEADS = 8
A_HEAD_DIM = 64
A_WIDTH = A_HEADS * A_HEAD_DIM
A_LEFT_CHUNKS = 8
REL_CLIP = 128
B_HEADS = 8
B_NOPE_DIM = 64
B_ROPE_DIM = 32
B_QK_DIM = B_NOPE_DIM + B_ROPE_DIM
B_V_DIM = 64
B_WIDTH = B_HEADS * B_V_DIM
Q_LORA = 256
KV_LORA = 128
ROPE_THETA = 10000.0
DEEPNORM_ALPHA = 2.0 ** 0.25
LN_EPS = 1e-5
RMS_EPS = 1e-6
NEG_INF = -1e30

LANES = 128
VMEM_LIMIT = 56 * 1024 * 1024

PROJ_TM = 512
A_TQ = 256
A_WIN = A_TQ + A_LEFT_CHUNKS * CHUNK
B_TQ = 1024
B_TK = 256
B_NB = B_TQ // B_TK
B_VT_ROWS = 80
LOG2E = 1.4426950408889634
OUT_TM = 256

BF16 = jnp.bfloat16
F32 = jnp.float32


def _layer_norm(x, g, b):
    mu = jnp.mean(x, axis=-1, keepdims=True)
    xc = x - mu
    var = jnp.mean(xc * xc, axis=-1, keepdims=True)
    return xc * lax.rsqrt(var + LN_EPS) * g + b


def _rms_norm(x, g):
    return x * lax.rsqrt(jnp.mean(x * x, axis=-1, keepdims=True) + RMS_EPS) * g


def _const_spec(shape):
    nd = len(shape)
    return pl.BlockSpec(shape, lambda *_: (0,) * nd, pipeline_mode=pl.Buffered(1))


def _proj_kernel(x_ref, pos_ref, lng_ref, lnb_ref, w1_ref, b1_ref, qg_ref, kvg_ref, wq_ref, wk_ref, wvt_ref,
                 freq_ref, aq_ref, ak_ref, av_ref, qb_ref, kb_ref, vbt_ref):
    h = _layer_norm(x_ref[0], lng_ref[...], lnb_ref[...])
    proj = jnp.dot(h.astype(BF16), w1_ref[...], preferred_element_type=F32) + b1_ref[...]
    aq_ref[0] = proj[:, 0:A_WIDTH].astype(BF16)
    ak_ref[0] = proj[:, A_WIDTH:2 * A_WIDTH].astype(BF16)
    av_ref[0] = proj[:, 2 * A_WIDTH:3 * A_WIDTH].astype(BF16)
    o = 3 * A_WIDTH
    cq = _rms_norm(proj[:, o:o + Q_LORA], qg_ref[...]).astype(BF16)
    o += Q_LORA
    ckv = _rms_norm(proj[:, o:o + KV_LORA], kvg_ref[...]).astype(BF16)
    o += KV_LORA
    kr = proj[:, o:o + LANES]
    kr_rot = proj[:, o + LANES:o + 2 * LANES]

    ang = pos_ref[0].astype(F32) * freq_ref[...]
    cos_t = jnp.cos(ang)
    sin_t = jnp.sin(ang)
    lane = lax.broadcasted_iota(jnp.int32, (1, LANES), 1)
    scale = B_QK_DIM ** -0.5 * LOG2E
    tq = jnp.where(lane < B_NOPE_DIM, scale, jnp.where(lane < B_QK_DIM, cos_t, sin_t) * scale)
    k_rope = kr * cos_t + kr_rot * sin_t

    q_all = jnp.dot(cq, wq_ref[...], preferred_element_type=F32)
    k_all = jnp.dot(ckv, wk_ref[...], preferred_element_type=F32)
    for hd in range(B_HEADS):
        sl = slice(hd * LANES, (hd + 1) * LANES)
        qb_ref[0, :, sl] = (q_all[:, sl] * tq).astype(BF16)
        kb_ref[0, :, sl] = (k_all[:, sl] + k_rope).astype(BF16)
    vt = lax.dot_general(wvt_ref[...], ckv, (((1,), (1,)), ((), ())), preferred_element_type=F32)
    tail = B_VT_ROWS - B_V_DIM
    ones_rows = (lax.broadcasted_iota(jnp.int32, (tail, vt.shape[1]), 0) == 0).astype(BF16)
    for hd in range(B_HEADS):
        vbt_ref[0, hd * B_VT_ROWS:hd * B_VT_ROWS + B_V_DIM, :] = vt[hd * B_V_DIM:(hd + 1) * B_V_DIM].astype(BF16)
        vbt_ref[0, hd * B_VT_ROWS + B_V_DIM:(hd + 1) * B_VT_ROWS, :] = ones_rows


def _token_projections(x, pos3, lng, lnb, w1, b1, qg, kvg, wq, wk, wvt, freq):
    B, S, _ = x.shape
    tm = PROJ_TM
    n1 = w1.shape[1]
    row = lambda w: pl.BlockSpec((1, tm, w), lambda b, i: (b, i, 0))
    return pl.pallas_call(
        _proj_kernel,
        grid=(B, S // tm),
        in_specs=[row(D_MODEL), row(1),
                  _const_spec((1, D_MODEL)), _const_spec((1, D_MODEL)),
                  _const_spec((D_MODEL, n1)), _const_spec((1, n1)),
                  _const_spec((1, Q_LORA)), _const_spec((1, KV_LORA)),
                  _const_spec((Q_LORA, B_HEADS * LANES)), _const_spec((KV_LORA, B_HEADS * LANES)),
                  _const_spec((B_WIDTH, KV_LORA)), _const_spec((1, LANES))],
        out_specs=[row(A_WIDTH), row(A_WIDTH), row(A_WIDTH), row(B_HEADS * LANES), row(B_HEADS * LANES),
                   pl.BlockSpec((1, B_HEADS * B_VT_ROWS, tm), lambda b, i: (b, 0, i))],
        out_shape=[jax.ShapeDtypeStruct((B, S, A_WIDTH), BF16)] * 3
                  + [jax.ShapeDtypeStruct((B, S, B_HEADS * LANES), BF16)] * 2
                  + [jax.ShapeDtypeStruct((B, B_HEADS * B_VT_ROWS, S), BF16)],
        compiler_params=pltpu.CompilerParams(dimension_semantics=("arbitrary", "arbitrary"),
                                             vmem_limit_bytes=VMEM_LIMIT),
        name="token_projections",
    )(x, pos3, lng, lnb, w1, b1, qg, kvg, wq, wk, wvt, freq)


def _mixer_a_kernel(q_ref, k2_ref, k1_ref, k0_ref, v2_ref, v1_ref, v0_ref, bias_ref, o_ref):
    i = pl.program_id(1)
    q = q_ref[0]
    k = jnp.concatenate([k2_ref[0], k1_ref[0], k0_ref[0]], axis=0)
    v = jnp.concatenate([v2_ref[0], v1_ref[0], v0_ref[0]], axis=0)
    kpos = lax.broadcasted_iota(jnp.int32, (1, A_WIN), 1) + (i * A_TQ - A_LEFT_CHUNKS * CHUNK)
    pad_mask = jnp.where(kpos >= 0, 0.0, NEG_INF).astype(F32)
    lane = lax.broadcasted_iota(jnp.int32, (1, LANES), 1)
    for pair in range(A_HEADS // 2):
        sl = slice(pair * LANES, (pair + 1) * LANES)
        qp, kp, vp = q[:, sl], k[:, sl], v[:, sl]
        out = None
        for hh in range(2):
            head_lanes = (lane // A_HEAD_DIM) == hh
            qm = jnp.where(head_lanes, qp, jnp.zeros_like(qp))
            vm = jnp.where(head_lanes, vp, jnp.zeros_like(vp))
            s = lax.dot_general(qm, kp, (((1,), (1,)), ((), ())), preferred_element_type=F32)
            s = s + bias_ref[2 * pair + hh] + pad_mask
            mx = jnp.max(s, axis=-1, keepdims=True)
            e = jnp.exp(s - mx)
            l = jnp.sum(e, axis=-1, keepdims=True)
            o = jnp.dot(e.astype(BF16), vm, preferred_element_type=F32) * (1.0 / l)
            out = o if out is None else out + o
        o_ref[0, :, sl] = out.astype(BF16)


def _mixer_a(aq, ak, av, bias):
    B, S, W = aq.shape
    tq = A_TQ
    blk = lambda back: pl.BlockSpec((1, tq, W), lambda b, i: (b, jnp.maximum(i - back, 0), 0))
    return pl.pallas_call(
        _mixer_a_kernel,
        grid=(B, S // tq),
        in_specs=[blk(0), blk(2), blk(1), blk(0), blk(2), blk(1), blk(0), _const_spec(bias.shape)],
        out_specs=blk(0),
        out_shape=jax.ShapeDtypeStruct((B, S, W), BF16),
        compiler_params=pltpu.CompilerParams(dimension_semantics=("arbitrary", "arbitrary"),
                                             vmem_limit_bytes=VMEM_LIMIT),
        name="mixer_a",
    )(aq, ak, ak, ak, av, av, av, bias)


A_ROLL = 1024


def _bias_kernel(row_ref, o_ref):
    rows = jnp.broadcast_to(row_ref[0], (A_TQ, A_ROLL))
    table = pltpu.roll(rows, 0, 1, stride=1, stride_axis=0)[:, :A_WIN]
    qc = lax.broadcasted_iota(jnp.int32, (A_TQ, A_WIN), 0) // CHUNK
    kc = lax.broadcasted_iota(jnp.int32, (A_TQ, A_WIN), 1) // CHUNK
    gap = qc + A_LEFT_CHUNKS - kc
    o_ref[0] = jnp.where((gap >= 0) & (gap <= A_LEFT_CHUNKS), table, NEG_INF)


def _mixer_a_bias(rel_bias):
    tbl = rel_bias.T.astype(F32)
    far_left = A_LEFT_CHUNKS * CHUNK - REL_CLIP
    hi = jnp.broadcast_to(tbl[:, -1:], (A_HEADS, far_left))
    lo = jnp.broadcast_to(tbl[:, :1], (A_HEADS, A_WIN - far_left - tbl.shape[1]))
    wrap = jnp.broadcast_to(tbl[:, -1:], (A_HEADS, A_ROLL - A_WIN))
    row = jnp.concatenate([hi, tbl[:, ::-1], lo, wrap], axis=1)[:, None, :]
    return pl.pallas_call(
        _bias_kernel,
        grid=(A_HEADS,),
        in_specs=[pl.BlockSpec((1, 1, A_ROLL), lambda h: (h, 0, 0))],
        out_specs=pl.BlockSpec((1, A_TQ, A_WIN), lambda h: (h, 0, 0)),
        out_shape=jax.ShapeDtypeStruct((A_HEADS, A_TQ, A_WIN), F32),
        name="mixer_a_bias",
    )(row)


def _mixer_b_kernel(q_ref, k_ref, vt_ref, o_ref, sa_ref, sb_ref, mxa_ref, mxb_ref, m_ref, acc_ref):
    qi = pl.program_id(2)
    q = q_ref[0]
    m_ref[...] = jnp.full(m_ref.shape, NEG_INF, F32)
    acc_ref[...] = jnp.zeros(acc_ref.shape, F32)

    def scores(blk, s_ref, mx_ref):
        start = pl.multiple_of(blk * B_TK, B_TK)
        k = k_ref[0, pl.ds(start, B_TK), :]
        s = lax.dot_general(k, q, (((1,), (1,)), ((), ())), preferred_element_type=F32)
        s_ref[...] = s
        mx_ref[...] = jnp.max(s, axis=0, keepdims=True)

    def accumulate(blk, s, mx):
        start = pl.multiple_of(blk * B_TK, B_TK)
        m_old = m_ref[...]
        m_new = jnp.maximum(m_old, mx)
        alpha = jnp.exp2(m_old - m_new)
        p = jnp.exp2(s - m_new).astype(BF16)
        vt = vt_ref[0, :, pl.ds(start, B_TK)]
        acc_ref[...] = alpha * acc_ref[...] + jnp.dot(vt, p, preferred_element_type=F32)
        m_ref[...] = m_new

    def diagonal_mask(s, key_offset):
        kc = (lax.broadcasted_iota(jnp.int32, (B_TK, B_TQ), 0) + key_offset) // CHUNK
        qc = lax.broadcasted_iota(jnp.int32, (B_TK, B_TQ), 1) // CHUNK
        return jnp.where(kc <= qc, s, NEG_INF)

    slots = ((sa_ref, mxa_ref), (sb_ref, mxb_ref))
    scores(0, *slots[0])

    def body(t, carry):
        for u in range(B_NB):
            s_ref, mx_ref = slots[u % 2]
            scores(t * B_NB + u + 1, *slots[(u + 1) % 2])
            accumulate(t * B_NB + u, s_ref[...], mx_ref[...])
        return carry

    lax.fori_loop(0, qi, body, 0)
    for u in range(B_NB):
        s_ref, _ = slots[u % 2]
        if u + 1 < B_NB:
            scores(qi * B_NB + u + 1, *slots[(u + 1) % 2])
        s = diagonal_mask(s_ref[...], u * B_TK)
        accumulate(qi * B_NB + u, s, jnp.max(s, axis=0, keepdims=True))
    denom = acc_ref[B_V_DIM:B_V_DIM + 1, :]
    o_ref[0] = (acc_ref[0:B_V_DIM, :] * (1.0 / denom)).astype(BF16)


def _mixer_b(qb, kb, vbt):
    B, S, _ = qb.shape
    return pl.pallas_call(
        _mixer_b_kernel,
        grid=(B, B_HEADS, S // B_TQ),
        in_specs=[pl.BlockSpec((1, B_TQ, LANES), lambda b, h, i: (b, i, h)),
                  pl.BlockSpec((1, S, LANES), lambda b, h, i: (b, 0, h)),
                  pl.BlockSpec((1, B_VT_ROWS, S), lambda b, h, i: (b, h, 0))],
        out_specs=pl.BlockSpec((1, B_V_DIM, B_TQ), lambda b, h, i: (b, h, i)),
        out_shape=jax.ShapeDtypeStruct((B, B_WIDTH, S), BF16),
        scratch_shapes=[pltpu.VMEM((B_TK, B_TQ), F32), pltpu.VMEM((B_TK, B_TQ), F32),
                        pltpu.VMEM((1, B_TQ), F32), pltpu.VMEM((1, B_TQ), F32),
                        pltpu.VMEM((1, B_TQ), F32), pltpu.VMEM((B_VT_ROWS, B_TQ), F32)],
        compiler_params=pltpu.CompilerParams(dimension_semantics=("arbitrary", "arbitrary", "arbitrary"),
                                             vmem_limit_bytes=VMEM_LIMIT),
        name="mixer_b",
    )(qb, kb, vbt)


def _merge_kernel(x_ref, ya_ref, ybt_ref, lng_ref, lnb_ref, wzg_ref, bzg_ref, wpa_ref, wpb_ref, wout_ref,
                  pg_ref, pb_ref, o_ref):
    h = _layer_norm(x_ref[0], lng_ref[...], lnb_ref[...])
    zg = jnp.dot(h.astype(BF16), wzg_ref[...], preferred_element_type=F32) + bzg_ref[...]
    za = zg[:, 0:A_WIDTH]
    zb = zg[:, A_WIDTH:A_WIDTH + B_WIDTH]
    ga = zg[:, A_WIDTH + B_WIDTH:A_WIDTH + B_WIDTH + D_MODEL]
    gb = zg[:, A_WIDTH + B_WIDTH + D_MODEL:]
    ya_in = ya_ref[0].astype(F32) * (za * jax.nn.sigmoid(za))
    yb_in = ybt_ref[0].astype(F32).T * (zb * jax.nn.sigmoid(zb))
    ya = jnp.dot(ya_in.astype(BF16), wpa_ref[...], preferred_element_type=F32)
    yb = jnp.dot(yb_in.astype(BF16), wpb_ref[...], preferred_element_type=F32)
    mixed = jax.nn.sigmoid(ga) * ya + jax.nn.sigmoid(gb) * yb
    out = jnp.dot(mixed.astype(BF16), wout_ref[...], preferred_element_type=F32)
    o_ref[0] = _layer_norm(DEEPNORM_ALPHA * h + out, pg_ref[...], pb_ref[...])


def _merge(x, ya, ybt, lng, lnb, wzg, bzg, wpa, wpb, wout, pg, pb):
    B, S, _ = x.shape
    tm = OUT_TM
    row = lambda w: pl.BlockSpec((1, tm, w), lambda b, i: (b, i, 0))
    nzg = wzg.shape[1]
    return pl.pallas_call(
        _merge_kernel,
        grid=(B, S // tm),
        in_specs=[row(D_MODEL), row(A_WIDTH), pl.BlockSpec((1, B_WIDTH, tm), lambda b, i: (b, 0, i)),
                  _const_spec((1, D_MODEL)), _const_spec((1, D_MODEL)),
                  _const_spec((D_MODEL, nzg)), _const_spec((1, nzg)),
                  _const_spec((A_WIDTH, D_MODEL)), _const_spec((B_WIDTH, D_MODEL)),
                  _const_spec((D_MODEL, D_MODEL)), _const_spec((1, D_MODEL)), _const_spec((1, D_MODEL))],
        out_specs=row(D_MODEL),
        out_shape=jax.ShapeDtypeStruct((B, S, D_MODEL), F32),
        compiler_params=pltpu.CompilerParams(dimension_semantics=("arbitrary", "arbitrary"),
                                             vmem_limit_bytes=VMEM_LIMIT),
        name="merge_out",
    )(x, ya, ybt, lng, lnb, wzg, bzg, wpa, wpb, wout, pg, pb)


def _rot_cols(w):
    half = w.shape[-1] // 2
    return jnp.concatenate([-w[..., half:], w[..., :half]], axis=-1)


def _prep_layer(w_in, b_in, w_uq, w_ukv):
    c = 0
    cols = {}
    for name, width in (("aq", A_WIDTH), ("ak", A_WIDTH), ("av", A_WIDTH), ("az", A_WIDTH), ("cq", Q_LORA),
                        ("ckv", KV_LORA), ("kr", B_ROPE_DIM), ("bz", B_WIDTH), ("ga", D_MODEL), ("gb", D_MODEL)):
        cols[name] = slice(c, c + width)
        c += width
    w = lambda n: w_in[:, cols[n]]
    b = lambda n: b_in[cols[n]]
    a_scale = A_HEAD_DIM ** -0.5
    zeros_w = jnp.zeros((D_MODEL, B_NOPE_DIM), F32)
    zeros_b = jnp.zeros((B_NOPE_DIM,), F32)
    w1 = jnp.concatenate([w("aq") * a_scale, w("ak"), w("av"), w("cq"), w("ckv"),
                          zeros_w, w("kr"), w("kr"),
                          zeros_w, _rot_cols(w("kr")), _rot_cols(w("kr"))], axis=1)
    b1 = jnp.concatenate([b("aq") * a_scale, b("ak"), b("av"), b("cq"), b("ckv"),
                          zeros_b, b("kr"), b("kr"),
                          zeros_b, _rot_cols(b("kr")), _rot_cols(b("kr"))])
    wzg = jnp.concatenate([w("az"), w("bz"), w("ga"), w("gb")], axis=1)
    bzg = jnp.concatenate([b("az"), b("bz"), b("ga"), b("gb")])

    uq = w_uq.reshape(Q_LORA, B_HEADS, B_QK_DIM)
    uq_rope = uq[:, :, B_NOPE_DIM:]
    wq = jnp.concatenate([uq, _rot_cols(uq_rope)], axis=-1).reshape(Q_LORA, B_HEADS * LANES)
    ukv = w_ukv.reshape(KV_LORA, B_HEADS, B_NOPE_DIM + B_V_DIM)
    wk = jnp.concatenate([ukv[:, :, :B_NOPE_DIM], jnp.zeros((KV_LORA, B_HEADS, LANES - B_NOPE_DIM), F32)],
                         axis=-1).reshape(KV_LORA, B_HEADS * LANES)
    wvt = ukv[:, :, B_NOPE_DIM:].reshape(KV_LORA, B_WIDTH).T
    return (w1.astype(BF16), b1[None, :], wq.astype(BF16), wk.astype(BF16), wvt.astype(BF16),
            wzg.astype(BF16), bzg[None, :])


def _rope_freq_row():
    half = B_ROPE_DIM // 2
    inv_freq = ROPE_THETA ** (-jnp.arange(half, dtype=F32) / half)
    return jnp.concatenate([jnp.zeros((B_NOPE_DIM,), F32), inv_freq, inv_freq, inv_freq, inv_freq])[None, :]


def kernel(x, positions, ln_in_g, ln_in_b, w_in, b_in, q_norm_g, kv_norm_g, w_uq, w_ukv, rel_bias, w_proj_a,
           w_proj_b, w_out, ln_post_g, ln_post_b):
    depth = w_in.shape[0]
    assert depth == 1, "the trunk-entry norm is recomputed per kernel, which is only valid for one layer"
    B, S, _ = x.shape
    pos3 = positions.reshape(B, S, 1)
    lng, lnb = ln_in_g[None, :], ln_in_b[None, :]
    freq = _rope_freq_row()
    l = 0
    w1, b1, wq, wk, wvt, wzg, bzg = _prep_layer(w_in[l], b_in[l], w_uq[l], w_ukv[l])
    aq, ak, av, qb, kb, vbt = _token_projections(x, pos3, lng, lnb, w1, b1, q_norm_g[l][None, :],
                                                 kv_norm_g[l][None, :], wq, wk, wvt, freq)
    ya = _mixer_a(aq, ak, av, _mixer_a_bias(rel_bias[l]))
    ybt = _mixer_b(qb, kb, vbt)
    return _merge(x, ya, ybt, lng, lnb, wzg, bzg, w_proj_a[l].astype(BF16), w_proj_b[l].astype(BF16),
                  w_out[l].astype(BF16), ln_post_g[l][None, :], ln_post_b[l][None, :])
```

```python
import functools

import jax
import jax.numpy as jnp
from jax import lax
from jax.experimental import pallas as pl
from jax.experimental.pallas import tpu as pltpu

D_MODEL = 1024
CHUNK = 64
A_HEADS = 8
A_HEAD_DIM = 64
A_WIDTH = A_HEADS * A_HEAD_DIM
A_LEFT_CHUNKS = 8
REL_CLIP = 128
B_HEADS = 8
B_NOPE_DIM = 64
B_ROPE_DIM = 32
B_QK_DIM = B_NOPE_DIM + B_ROPE_DIM
B_V_DIM = 64
B_WIDTH = B_HEADS * B_V_DIM
Q_LORA = 256
KV_LORA = 128
ROPE_THETA = 10000.0
DEEPNORM_ALPHA = 2.0 ** 0.25
LN_EPS = 1e-5
RMS_EPS = 1e-6
NEG_INF = -1e30

LANES = 128
VMEM_LIMIT = 56 * 1024 * 1024

PROJ_TM = 512
A_TQ = 256
A_WIN = A_TQ + A_LEFT_CHUNKS * CHUNK
B_TQ = 1024
B_TK = 256
B_NB = B_TQ // B_TK
B_AHEAD = 2
B_VT_ROWS = 80
LOG2E = 1.4426950408889634
OUT_TM = 256

BF16 = jnp.bfloat16
F32 = jnp.float32


def _layer_norm(x, g, b):
    mu = jnp.mean(x, axis=-1, keepdims=True)
    xc = x - mu
    var = jnp.mean(xc * xc, axis=-1, keepdims=True)
    return xc * lax.rsqrt(var + LN_EPS) * g + b


def _rms_norm(x, g):
    return x * lax.rsqrt(jnp.mean(x * x, axis=-1, keepdims=True) + RMS_EPS) * g


def _const_spec(shape):
    nd = len(shape)
    return pl.BlockSpec(shape, lambda *_: (0,) * nd, pipeline_mode=pl.Buffered(1))


def _proj_kernel(x_ref, pos_ref, lng_ref, lnb_ref, w1_ref, b1_ref, qg_ref, kvg_ref, wq_ref, wk_ref, wvt_ref,
                 freq_ref, aq_ref, ak_ref, av_ref, qb_ref, kb_ref, vbt_ref):
    h = _layer_norm(x_ref[0], lng_ref[...], lnb_ref[...])
    proj = jnp.dot(h.astype(BF16), w1_ref[...], preferred_element_type=F32) + b1_ref[...]
    aq_ref[0] = proj[:, 0:A_WIDTH].astype(BF16)
    ak_ref[0] = proj[:, A_WIDTH:2 * A_WIDTH].astype(BF16)
    av_ref[0] = proj[:, 2 * A_WIDTH:3 * A_WIDTH].astype(BF16)
    o = 3 * A_WIDTH
    cq = _rms_norm(proj[:, o:o + Q_LORA], qg_ref[...]).astype(BF16)
    o += Q_LORA
    ckv = _rms_norm(proj[:, o:o + KV_LORA], kvg_ref[...]).astype(BF16)
    o += KV_LORA
    kr = proj[:, o:o + LANES]
    kr_rot = proj[:, o + LANES:o + 2 * LANES]

    ang = pos_ref[0].astype(F32) * freq_ref[...]
    cos_t = jnp.cos(ang)
    sin_t = jnp.sin(ang)
    lane = lax.broadcasted_iota(jnp.int32, (1, LANES), 1)
    scale = B_QK_DIM ** -0.5 * LOG2E
    tq = jnp.where(lane < B_NOPE_DIM, scale, jnp.where(lane < B_QK_DIM, cos_t, sin_t) * scale)
    k_rope = kr * cos_t + kr_rot * sin_t

    q_all = jnp.dot(cq, wq_ref[...], preferred_element_type=F32)
    k_all = jnp.dot(ckv, wk_ref[...], preferred_element_type=F32)
    for hd in range(B_HEADS):
        sl = slice(hd * LANES, (hd + 1) * LANES)
        qb_ref[0, :, sl] = (q_all[:, sl] * tq).astype(BF16)
        kb_ref[0, :, sl] = (k_all[:, sl] + k_rope).astype(BF16)
    vt = lax.dot_general(wvt_ref[...], ckv, (((1,), (1,)), ((), ())), preferred_element_type=F32)
    tail = B_VT_ROWS - B_V_DIM
    ones_rows = (lax.broadcasted_iota(jnp.int32, (tail, vt.shape[1]), 0) == 0).astype(BF16)
    for hd in range(B_HEADS):
        vbt_ref[0, hd * B_VT_ROWS:hd * B_VT_ROWS + B_V_DIM, :] = vt[hd * B_V_DIM:(hd + 1) * B_V_DIM].astype(BF16)
        vbt_ref[0, hd * B_VT_ROWS + B_V_DIM:(hd + 1) * B_VT_ROWS, :] = ones_rows


def _token_projections(x, pos3, lng, lnb, w1, b1, qg, kvg, wq, wk, wvt, freq):
    B, S, _ = x.shape
    tm = PROJ_TM
    n1 = w1.shape[1]
    row = lambda w: pl.BlockSpec((1, tm, w), lambda b, i: (b, i, 0))
    return pl.pallas_call(
        _proj_kernel,
        grid=(B, S // tm),
        in_specs=[row(D_MODEL), row(1),
                  _const_spec((1, D_MODEL)), _const_spec((1, D_MODEL)),
                  _const_spec((D_MODEL, n1)), _const_spec((1, n1)),
                  _const_spec((1, Q_LORA)), _const_spec((1, KV_LORA)),
                  _const_spec((Q_LORA, B_HEADS * LANES)), _const_spec((KV_LORA, B_HEADS * LANES)),
                  _const_spec((B_WIDTH, KV_LORA)), _const_spec((1, LANES))],
        out_specs=[row(A_WIDTH), row(A_WIDTH), row(A_WIDTH), row(B_HEADS * LANES), row(B_HEADS * LANES),
                   pl.BlockSpec((1, B_HEADS * B_VT_ROWS, tm), lambda b, i: (b, 0, i))],
        out_shape=[jax.ShapeDtypeStruct((B, S, A_WIDTH), BF16)] * 3
                  + [jax.ShapeDtypeStruct((B, S, B_HEADS * LANES), BF16)] * 2
                  + [jax.ShapeDtypeStruct((B, B_HEADS * B_VT_ROWS, S), BF16)],
        compiler_params=pltpu.CompilerParams(dimension_semantics=("arbitrary", "arbitrary"),
                                             vmem_limit_bytes=VMEM_LIMIT),
        name="token_projections",
    )(x, pos3, lng, lnb, w1, b1, qg, kvg, wq, wk, wvt, freq)


def _mixer_a_kernel(q_ref, k2_ref, k1_ref, k0_ref, v2_ref, v1_ref, v0_ref, bias_ref, o_ref):
    i = pl.program_id(1)
    q = q_ref[0]
    k = jnp.concatenate([k2_ref[0], k1_ref[0], k0_ref[0]], axis=0)
    v = jnp.concatenate([v2_ref[0], v1_ref[0], v0_ref[0]], axis=0)
    kpos = lax.broadcasted_iota(jnp.int32, (1, A_WIN), 1) + (i * A_TQ - A_LEFT_CHUNKS * CHUNK)
    pad_mask = jnp.where(kpos >= 0, 0.0, NEG_INF).astype(F32)
    lane = lax.broadcasted_iota(jnp.int32, (1, LANES), 1)
    for pair in range(A_HEADS // 2):
        sl = slice(pair * LANES, (pair + 1) * LANES)
        qp, kp, vp = q[:, sl], k[:, sl], v[:, sl]
        out = None
        for hh in range(2):
            head_lanes = (lane // A_HEAD_DIM) == hh
            qm = jnp.where(head_lanes, qp, jnp.zeros_like(qp))
            vm = jnp.where(head_lanes, vp, jnp.zeros_like(vp))
            s = lax.dot_general(qm, kp, (((1,), (1,)), ((), ())), preferred_element_type=F32)
            s = s + bias_ref[2 * pair + hh] + pad_mask
            mx = jnp.max(s, axis=-1, keepdims=True)
            e = jnp.exp(s - mx)
            l = jnp.sum(e, axis=-1, keepdims=True)
            o = jnp.dot(e.astype(BF16), vm, preferred_element_type=F32) * (1.0 / l)
            out = o if out is None else out + o
        o_ref[0, :, sl] = out.astype(BF16)


def _mixer_a(aq, ak, av, bias):
    B, S, W = aq.shape
    tq = A_TQ
    blk = lambda back: pl.BlockSpec((1, tq, W), lambda b, i: (b, jnp.maximum(i - back, 0), 0))
    return pl.pallas_call(
        _mixer_a_kernel,
        grid=(B, S // tq),
        in_specs=[blk(0), blk(2), blk(1), blk(0), blk(2), blk(1), blk(0), _const_spec(bias.shape)],
        out_specs=blk(0),
        out_shape=jax.ShapeDtypeStruct((B, S, W), BF16),
        compiler_params=pltpu.CompilerParams(dimension_semantics=("arbitrary", "arbitrary"),
                                             vmem_limit_bytes=VMEM_LIMIT),
        name="mixer_a",
    )(aq, ak, ak, ak, av, av, av, bias)


A_ROLL = 1024


def _bias_kernel(row_ref, o_ref):
    rows = jnp.broadcast_to(row_ref[0], (A_TQ, A_ROLL))
    table = pltpu.roll(rows, 0, 1, stride=1, stride_axis=0)[:, :A_WIN]
    qc = lax.broadcasted_iota(jnp.int32, (A_TQ, A_WIN), 0) // CHUNK
    kc = lax.broadcasted_iota(jnp.int32, (A_TQ, A_WIN), 1) // CHUNK
    gap = qc + A_LEFT_CHUNKS - kc
    o_ref[0] = jnp.where((gap >= 0) & (gap <= A_LEFT_CHUNKS), table, NEG_INF)


def _mixer_a_bias(rel_bias):
    tbl = rel_bias.T.astype(F32)
    far_left = A_LEFT_CHUNKS * CHUNK - REL_CLIP
    hi = jnp.broadcast_to(tbl[:, -1:], (A_HEADS, far_left))
    lo = jnp.broadcast_to(tbl[:, :1], (A_HEADS, A_WIN - far_left - tbl.shape[1]))
    wrap = jnp.broadcast_to(tbl[:, -1:], (A_HEADS, A_ROLL - A_WIN))
    row = jnp.concatenate([hi, tbl[:, ::-1], lo, wrap], axis=1)[:, None, :]
    return pl.pallas_call(
        _bias_kernel,
        grid=(A_HEADS,),
        in_specs=[pl.BlockSpec((1, 1, A_ROLL), lambda h: (h, 0, 0))],
        out_specs=pl.BlockSpec((1, A_TQ, A_WIN), lambda h: (h, 0, 0)),
        out_shape=jax.ShapeDtypeStruct((A_HEADS, A_TQ, A_WIN), F32),
        name="mixer_a_bias",
    )(row)


def _mixer_b_kernel(q_ref, k_ref, vt_ref, o_ref, *scratch):
    s_refs = scratch[0:B_NB]
    mx_refs = scratch[B_NB:2 * B_NB]
    m_ref, acc_ref = scratch[2 * B_NB:]
    qi = pl.program_id(2)
    m_ref[...] = jnp.full(m_ref.shape, NEG_INF, F32)
    acc_ref[...] = jnp.zeros(acc_ref.shape, F32)

    def scores(blk, slot, lo=0):
        start = pl.multiple_of(blk * B_TK, B_TK)
        k = k_ref[0, pl.ds(start, B_TK), :]
        q = q_ref[0, lo:, :]
        s = lax.dot_general(k, q, (((1,), (1,)), ((), ())), preferred_element_type=F32)
        s_refs[slot][:, lo:] = s
        mx_refs[slot][:, lo:] = jnp.max(s, axis=0, keepdims=True)

    def accumulate(blk, s, mx, lo=0):
        start = pl.multiple_of(blk * B_TK, B_TK)
        m_old = m_ref[:, lo:]
        m_new = jnp.maximum(m_old, mx)
        alpha = jnp.exp2(m_old - m_new)
        p = jnp.exp2(s - m_new).astype(BF16)
        vt = vt_ref[0, :, pl.ds(start, B_TK)]
        acc_ref[:, lo:] = alpha * acc_ref[:, lo:] + jnp.dot(vt, p, preferred_element_type=F32)
        m_ref[:, lo:] = m_new

    def diagonal_mask(s):
        kc = lax.broadcasted_iota(jnp.int32, s.shape, 0) // CHUNK
        qc = lax.broadcasted_iota(jnp.int32, s.shape, 1) // CHUNK
        return jnp.where(kc <= qc, s, NEG_INF)

    for u in range(B_AHEAD):
        scores(u, u)

    def body(t, carry):
        for u in range(B_NB):
            scores(t * B_NB + u + B_AHEAD, (u + B_AHEAD) % B_NB)
            accumulate(t * B_NB + u, s_refs[u][...], mx_refs[u][...])
        return carry

    lax.fori_loop(0, qi, body, 0)
    for u in range(B_NB):
        lo = u * B_TK
        if u + B_AHEAD < B_NB:
            scores(qi * B_NB + u + B_AHEAD, u + B_AHEAD, lo=(u + B_AHEAD) * B_TK)
        s = diagonal_mask(s_refs[u][:, lo:])
        accumulate(qi * B_NB + u, s, jnp.max(s, axis=0, keepdims=True), lo=lo)
    denom = acc_ref[B_V_DIM:B_V_DIM + 1, :]
    o_ref[0] = (acc_ref[0:B_V_DIM, :] * (1.0 / denom)).astype(BF16)


def _mixer_b(qb, kb, vbt):
    B, S, _ = qb.shape
    return pl.pallas_call(
        _mixer_b_kernel,
        grid=(B, B_HEADS, S // B_TQ),
        in_specs=[pl.BlockSpec((1, B_TQ, LANES), lambda b, h, i: (b, i, h)),
                  pl.BlockSpec((1, S, LANES), lambda b, h, i: (b, 0, h)),
                  pl.BlockSpec((1, B_VT_ROWS, S), lambda b, h, i: (b, h, 0))],
        out_specs=pl.BlockSpec((1, B_V_DIM, B_TQ), lambda b, h, i: (b, h, i)),
        out_shape=jax.ShapeDtypeStruct((B, B_WIDTH, S), BF16),
        scratch_shapes=[pltpu.VMEM((B_TK, B_TQ), F32)] * B_NB + [pltpu.VMEM((1, B_TQ), F32)] * B_NB
                       + [pltpu.VMEM((1, B_TQ), F32), pltpu.VMEM((B_VT_ROWS, B_TQ), F32)],
        compiler_params=pltpu.CompilerParams(dimension_semantics=("arbitrary", "arbitrary", "arbitrary"),
                                             vmem_limit_bytes=VMEM_LIMIT),
        name="mixer_b",
    )(qb, kb, vbt)


def _merge_kernel(x_ref, ya_ref, ybt_ref, lng_ref, lnb_ref, wzg_ref, bzg_ref, wpa_ref, wpb_ref, wout_ref,
                  pg_ref, pb_ref, o_ref):
    h = _layer_norm(x_ref[0], lng_ref[...], lnb_ref[...])
    zg = jnp.dot(h.astype(BF16), wzg_ref[...], preferred_element_type=F32) + bzg_ref[...]
    za = zg[:, 0:A_WIDTH]
    zb = zg[:, A_WIDTH:A_WIDTH + B_WIDTH]
    ga = zg[:, A_WIDTH + B_WIDTH:A_WIDTH + B_WIDTH + D_MODEL]
    gb = zg[:, A_WIDTH + B_WIDTH + D_MODEL:]
    ya_in = ya_ref[0].astype(F32) * (za * jax.nn.sigmoid(za))
    yb_in = ybt_ref[0].astype(F32).T * (zb * jax.nn.sigmoid(zb))
    ya = jnp.dot(ya_in.astype(BF16), wpa_ref[...], preferred_element_type=F32)
    yb = jnp.dot(yb_in.astype(BF16), wpb_ref[...], preferred_element_type=F32)
    mixed = jax.nn.sigmoid(ga) * ya + jax.nn.sigmoid(gb) * yb
    out = jnp.dot(mixed.astype(BF16), wout_ref[...], preferred_element_type=F32)
    o_ref[0] = _layer_norm(DEEPNORM_ALPHA * h + out, pg_ref[...], pb_ref[...])


def _merge(x, ya, ybt, lng, lnb, wzg, bzg, wpa, wpb, wout, pg, pb):
    B, S, _ = x.shape
    tm = OUT_TM
    row = lambda w: pl.BlockSpec((1, tm, w), lambda b, i: (b, i, 0))
    nzg = wzg.shape[1]
    return pl.pallas_call(
        _merge_kernel,
        grid=(B, S // tm),
        in_specs=[row(D_MODEL), row(A_WIDTH), pl.BlockSpec((1, B_WIDTH, tm), lambda b, i: (b, 0, i)),
                  _const_spec((1, D_MODEL)), _const_spec((1, D_MODEL)),
                  _const_spec((D_MODEL, nzg)), _const_spec((1, nzg)),
                  _const_spec((A_WIDTH, D_MODEL)), _const_spec((B_WIDTH, D_MODEL)),
                  _const_spec((D_MODEL, D_MODEL)), _const_spec((1, D_MODEL)), _const_spec((1, D_MODEL))],
        out_specs=row(D_MODEL),
        out_shape=jax.ShapeDtypeStruct((B, S, D_MODEL), F32),
        compiler_params=pltpu.CompilerParams(dimension_semantics=("arbitrary", "arbitrary"),
                                             vmem_limit_bytes=VMEM_LIMIT),
        name="merge_out",
    )(x, ya, ybt, lng, lnb, wzg, bzg, wpa, wpb, wout, pg, pb)


def _rot_cols(w):
    half = w.shape[-1] // 2
    return jnp.concatenate([-w[..., half:], w[..., :half]], axis=-1)


def _prep_layer(w_in, b_in, w_uq, w_ukv):
    c = 0
    cols = {}
    for name, width in (("aq", A_WIDTH), ("ak", A_WIDTH), ("av", A_WIDTH), ("az", A_WIDTH), ("cq", Q_LORA),
                        ("ckv", KV_LORA), ("kr", B_ROPE_DIM), ("bz", B_WIDTH), ("ga", D_MODEL), ("gb", D_MODEL)):
        cols[name] = slice(c, c + width)
        c += width
    w = lambda n: w_in[:, cols[n]]
    b = lambda n: b_in[cols[n]]
    a_scale = A_HEAD_DIM ** -0.5
    zeros_w = jnp.zeros((D_MODEL, B_NOPE_DIM), F32)
    zeros_b = jnp.zeros((B_NOPE_DIM,), F32)
    w1 = jnp.concatenate([w("aq") * a_scale, w("ak"), w("av"), w("cq"), w("ckv"),
                          zeros_w, w("kr"), w("kr"),
                          zeros_w, _rot_cols(w("kr")), _rot_cols(w("kr"))], axis=1)
    b1 = jnp.concatenate([b("aq") * a_scale, b("ak"), b("av"), b("cq"), b("ckv"),
                          zeros_b, b("kr"), b("kr"),
                          zeros_b, _rot_cols(b("kr")), _rot_cols(b("kr"))])
    wzg = jnp.concatenate([w("az"), w("bz"), w("ga"), w("gb")], axis=1)
    bzg = jnp.concatenate([b("az"), b("bz"), b("ga"), b("gb")])

    uq = w_uq.reshape(Q_LORA, B_HEADS, B_QK_DIM)
    uq_rope = uq[:, :, B_NOPE_DIM:]
    wq = jnp.concatenate([uq, _rot_cols(uq_rope)], axis=-1).reshape(Q_LORA, B_HEADS * LANES)
    ukv = w_ukv.reshape(KV_LORA, B_HEADS, B_NOPE_DIM + B_V_DIM)
    wk = jnp.concatenate([ukv[:, :, :B_NOPE_DIM], jnp.zeros((KV_LORA, B_HEADS, LANES - B_NOPE_DIM), F32)],
                         axis=-1).reshape(KV_LORA, B_HEADS * LANES)
    wvt = ukv[:, :, B_NOPE_DIM:].reshape(KV_LORA, B_WIDTH).T
    return (w1.astype(BF16), b1[None, :], wq.astype(BF16), wk.astype(BF16), wvt.astype(BF16),
            wzg.astype(BF16), bzg[None, :])


def _rope_freq_row():
    half = B_ROPE_DIM // 2
    inv_freq = ROPE_THETA ** (-jnp.arange(half, dtype=F32) / half)
    return jnp.concatenate([jnp.zeros((B_NOPE_DIM,), F32), inv_freq, inv_freq, inv_freq, inv_freq])[None, :]


def kernel(x, positions, ln_in_g, ln_in_b, w_in, b_in, q_norm_g, kv_norm_g, w_uq, w_ukv, rel_bias, w_proj_a,
           w_proj_b, w_out, ln_post_g, ln_post_b):
    depth = w_in.shape[0]
    assert depth == 1, "the trunk-entry norm is recomputed per kernel, which is only valid for one layer"
    B, S, _ = x.shape
    pos3 = positions.reshape(B, S, 1)
    lng, lnb = ln_in_g[None, :], ln_in_b[None, :]
    freq = _rope_freq_row()
    l = 0
    w1, b1, wq, wk, wvt, wzg, bzg = _prep_layer(w_in[l], b_in[l], w_uq[l], w_ukv[l])
    aq, ak, av, qb, kb, vbt = _token_projections(x, pos3, lng, lnb, w1, b1, q_norm_g[l][None, :],
                                                 kv_norm_g[l][None, :], wq, wk, wvt, freq)
    ya = _mixer_a(aq, ak, av, _mixer_a_bias(rel_bias[l]))
    ybt = _mixer_b(qb, kb, vbt)
    return _merge(x, ya, ybt, lng, lnb, wzg, bzg, w_proj_a[l].astype(BF16), w_proj_b[l].astype(BF16),
                  w_out[l].astype(BF16), ln_post_g[l][None, :], ln_post_b[l][None, :])
```

```python
import jax
import jax.numpy as jnp
from jax import lax
from jax.experimental import pallas as pl
from jax.experimental.pallas import tpu as pltpu

D_MODEL = 1024
CHUNK = 64
A_HEADS = 8
A_HEAD_DIM = 64
A_WIDTH = A_HEADS * A_HEAD_DIM
A_LEFT_CHUNKS = 8
REL_CLIP = 128
B_HEADS = 8
B_NOPE_DIM = 64
B_ROPE_DIM = 32
B_QK_DIM = B_NOPE_DIM + B_ROPE_DIM
B_V_DIM = 64
B_WIDTH = B_HEADS * B_V_DIM
Q_LORA = 256
KV_LORA = 128
ROPE_THETA = 10000.0
DEEPNORM_ALPHA = 2.0 ** 0.25
LN_EPS = 1e-5
RMS_EPS = 1e-6
NEG_INF = -1e30

LANES = 128
VMEM_LIMIT = 56 * 1024 * 1024

PROJ_TM = 512
A_TQ = 256
A_WIN = A_TQ + A_LEFT_CHUNKS * CHUNK
A_ROLL = 1024
B_TQ = 1024
B_TK = 256
B_NB = B_TQ // B_TK
B_AHEAD = 2
V_DIM = 64
VT_ROWS = 80
LOG2E = 1.4426950408889634
OUT_TM = 256

BF16 = jnp.bfloat16
F32 = jnp.float32


def _layer_norm(x, g, b):
    mu = jnp.mean(x, axis=-1, keepdims=True)
    xc = x - mu
    var = jnp.mean(xc * xc, axis=-1, keepdims=True)
    return xc * lax.rsqrt(var + LN_EPS) * g + b


def _rms_norm(x, g):
    return x * lax.rsqrt(jnp.mean(x * x, axis=-1, keepdims=True) + RMS_EPS) * g


def _const_spec(shape):
    nd = len(shape)
    return pl.BlockSpec(shape, lambda *_: (0,) * nd, pipeline_mode=pl.Buffered(1))


def _store_values_t(ref, vt):
    tail = VT_ROWS - V_DIM
    ones_rows = (lax.broadcasted_iota(jnp.int32, (tail, vt.shape[1]), 0) == 0).astype(BF16)
    for hd in range(vt.shape[0] // V_DIM):
        ref[0, hd * VT_ROWS:hd * VT_ROWS + V_DIM, :] = vt[hd * V_DIM:(hd + 1) * V_DIM].astype(BF16)
        ref[0, hd * VT_ROWS + V_DIM:(hd + 1) * VT_ROWS, :] = ones_rows


def _proj_kernel(x_ref, pos_ref, lng_ref, lnb_ref, w1_ref, b1_ref, wavt_ref, bav_ref, qg_ref, kvg_ref, wq_ref,
                 wk_ref, wvt_ref, freq_ref, aq_ref, ak_ref, avt_ref, qb_ref, kb_ref, vbt_ref):
    hb = _layer_norm(x_ref[0], lng_ref[...], lnb_ref[...]).astype(BF16)
    proj = jnp.dot(hb, w1_ref[...], preferred_element_type=F32) + b1_ref[...]
    aq_ref[0] = proj[:, 0:A_WIDTH].astype(BF16)
    ak_ref[0] = proj[:, A_WIDTH:2 * A_WIDTH].astype(BF16)
    avt = lax.dot_general(wavt_ref[...], hb, (((1,), (1,)), ((), ())), preferred_element_type=F32) + bav_ref[...]
    _store_values_t(avt_ref, avt)
    o = 2 * A_WIDTH
    cq = _rms_norm(proj[:, o:o + Q_LORA], qg_ref[...]).astype(BF16)
    o += Q_LORA
    ckv = _rms_norm(proj[:, o:o + KV_LORA], kvg_ref[...]).astype(BF16)
    o += KV_LORA
    kr = proj[:, o:o + LANES]
    kr_rot = proj[:, o + LANES:o + 2 * LANES]

    ang = pos_ref[0].astype(F32) * freq_ref[...]
    cos_t = jnp.cos(ang)
    sin_t = jnp.sin(ang)
    lane = lax.broadcasted_iota(jnp.int32, (1, LANES), 1)
    scale = B_QK_DIM ** -0.5 * LOG2E
    tq = jnp.where(lane < B_NOPE_DIM, scale, jnp.where(lane < B_QK_DIM, cos_t, sin_t) * scale)
    k_rope = kr * cos_t + kr_rot * sin_t

    q_all = jnp.dot(cq, wq_ref[...], preferred_element_type=F32)
    k_all = jnp.dot(ckv, wk_ref[...], preferred_element_type=F32)
    for hd in range(B_HEADS):
        sl = slice(hd * LANES, (hd + 1) * LANES)
        qb_ref[0, :, sl] = (q_all[:, sl] * tq).astype(BF16)
        kb_ref[0, :, sl] = (k_all[:, sl] + k_rope).astype(BF16)
    vt = lax.dot_general(wvt_ref[...], ckv, (((1,), (1,)), ((), ())), preferred_element_type=F32)
    _store_values_t(vbt_ref, vt)


def _token_projections(x, pos3, lng, lnb, w1, b1, wavt, bav, qg, kvg, wq, wk, wvt, freq):
    B, S, _ = x.shape
    tm = PROJ_TM
    n1 = w1.shape[1]
    row = lambda w: pl.BlockSpec((1, tm, w), lambda b, i: (b, i, 0))
    vt_spec = pl.BlockSpec((1, B_HEADS * VT_ROWS, tm), lambda b, i: (b, 0, i))
    vt_shape = jax.ShapeDtypeStruct((B, B_HEADS * VT_ROWS, S), BF16)
    return pl.pallas_call(
        _proj_kernel,
        grid=(B, S // tm),
        in_specs=[row(D_MODEL), row(1),
                  _const_spec((1, D_MODEL)), _const_spec((1, D_MODEL)),
                  _const_spec((D_MODEL, n1)), _const_spec((1, n1)),
                  _const_spec((A_WIDTH, D_MODEL)), _const_spec((A_WIDTH, 1)),
                  _const_spec((1, Q_LORA)), _const_spec((1, KV_LORA)),
                  _const_spec((Q_LORA, B_HEADS * LANES)), _const_spec((KV_LORA, B_HEADS * LANES)),
                  _const_spec((B_WIDTH, KV_LORA)), _const_spec((1, LANES))],
        out_specs=[row(A_WIDTH), row(A_WIDTH), vt_spec, row(B_HEADS * LANES), row(B_HEADS * LANES), vt_spec],
        out_shape=[jax.ShapeDtypeStruct((B, S, A_WIDTH), BF16)] * 2 + [vt_shape]
                  + [jax.ShapeDtypeStruct((B, S, B_HEADS * LANES), BF16)] * 2 + [vt_shape],
        compiler_params=pltpu.CompilerParams(dimension_semantics=("arbitrary", "arbitrary"),
                                             vmem_limit_bytes=VMEM_LIMIT),
        name="token_projections",
    )(x, pos3, lng, lnb, w1, b1, wavt, bav, qg, kvg, wq, wk, wvt, freq)


def _mixer_a_kernel(q_ref, k2_ref, k1_ref, k0_ref, v2_ref, v1_ref, v0_ref, bias_ref, o_ref,
                    s0_ref, s1_ref, mx0_ref, mx1_ref):
    i = pl.program_id(1)
    s_refs, mx_refs = (s0_ref, s1_ref), (mx0_ref, mx1_ref)
    n_old = A_LEFT_CHUNKS * CHUNK
    kpos = lax.broadcasted_iota(jnp.int32, (n_old, 1), 0) + (i * A_TQ - n_old)
    pad_mask = jnp.where(kpos >= 0, 0.0, NEG_INF).astype(F32)
    lane = lax.broadcasted_iota(jnp.int32, (1, LANES), 1)

    def scores(pair, slot):
        sl = slice(pair * LANES, (pair + 1) * LANES)
        qp = q_ref[0, :, sl]
        kp = jnp.concatenate([k2_ref[0, :, sl], k1_ref[0, :, sl], k0_ref[0, :, sl]], axis=0)
        q2 = jnp.concatenate([jnp.where((lane // A_HEAD_DIM) == hh, qp, jnp.zeros_like(qp)) for hh in range(2)],
                             axis=0)
        s = lax.dot_general(kp, q2, (((1,), (1,)), ((), ())), preferred_element_type=F32)
        s = s + jnp.concatenate([bias_ref[2 * pair], bias_ref[2 * pair + 1]], axis=1)
        s = jnp.concatenate([s[:n_old] + pad_mask, s[n_old:]], axis=0)
        s_refs[slot][...] = s
        mx_refs[slot][...] = jnp.max(s, axis=0, keepdims=True)

    def finish(pair, slot):
        p = jnp.exp2(s_refs[slot][...] - mx_refs[slot][...]).astype(BF16)
        rows = slice(2 * pair * VT_ROWS, (2 * pair + 2) * VT_ROWS)
        vt = jnp.concatenate([v2_ref[0, rows, :], v1_ref[0, rows, :], v0_ref[0, rows, :]], axis=1)
        o = jnp.dot(vt, p, preferred_element_type=F32)
        for hh in range(2):
            oh = o[hh * VT_ROWS:(hh + 1) * VT_ROWS, hh * A_TQ:(hh + 1) * A_TQ]
            hd = 2 * pair + hh
            o_ref[0, hd * V_DIM:(hd + 1) * V_DIM, :] = (oh[:V_DIM] * (1.0 / oh[V_DIM:V_DIM + 1])).astype(BF16)

    n_pairs = A_HEADS // 2
    scores(0, 0)
    for pair in range(n_pairs):
        if pair + 1 < n_pairs:
            scores(pair + 1, (pair + 1) % 2)
        finish(pair, pair % 2)


def _mixer_a(aq, ak, avt, bias):
    B, S, W = aq.shape
    tq = A_TQ
    blk = lambda back: pl.BlockSpec((1, tq, W), lambda b, i: (b, jnp.maximum(i - back, 0), 0))
    vblk = lambda back: pl.BlockSpec((1, A_HEADS * VT_ROWS, tq), lambda b, i: (b, 0, jnp.maximum(i - back, 0)))
    return pl.pallas_call(
        _mixer_a_kernel,
        grid=(B, S // tq),
        in_specs=[blk(0), blk(2), blk(1), blk(0), vblk(2), vblk(1), vblk(0), _const_spec(bias.shape)],
        out_specs=pl.BlockSpec((1, W, tq), lambda b, i: (b, 0, i)),
        out_shape=jax.ShapeDtypeStruct((B, W, S), BF16),
        scratch_shapes=[pltpu.VMEM((A_WIN, 2 * tq), F32)] * 2 + [pltpu.VMEM((1, 2 * tq), F32)] * 2,
        compiler_params=pltpu.CompilerParams(dimension_semantics=("arbitrary", "arbitrary"),
                                             vmem_limit_bytes=VMEM_LIMIT),
        name="mixer_a",
    )(aq, ak, ak, ak, avt, avt, avt, bias)


def _bias_kernel(row_ref, o_ref):
    rows = jnp.broadcast_to(row_ref[0], (A_WIN, A_ROLL))
    table = pltpu.roll(rows, 0, 1, stride=1, stride_axis=0)[:, :A_TQ]
    kc = lax.broadcasted_iota(jnp.int32, (A_WIN, A_TQ), 0) // CHUNK
    qc = lax.broadcasted_iota(jnp.int32, (A_WIN, A_TQ), 1) // CHUNK
    gap = qc + A_LEFT_CHUNKS - kc
    o_ref[0] = jnp.where((gap >= 0) & (gap <= A_LEFT_CHUNKS), table * LOG2E, NEG_INF)


def _mixer_a_bias(rel_bias):
    tbl = rel_bias.T.astype(F32)
    n_old = A_LEFT_CHUNKS * CHUNK
    first_tbl = A_ROLL - n_old - REL_CLIP
    far = jnp.broadcast_to(tbl[:, -1:], (A_HEADS, A_ROLL))
    near = jnp.broadcast_to(tbl[:, :1], (A_HEADS, A_ROLL))
    row = jnp.concatenate([far[:, :A_ROLL - A_WIN + 1], near[:, A_ROLL - A_WIN + 1:first_tbl], tbl,
                           far[:, first_tbl + tbl.shape[1]:]], axis=1)
    row = row[:, None, :]
    return pl.pallas_call(
        _bias_kernel,
        grid=(A_HEADS,),
        in_specs=[pl.BlockSpec((1, 1, A_ROLL), lambda h: (h, 0, 0))],
        out_specs=pl.BlockSpec((1, A_WIN, A_TQ), lambda h: (h, 0, 0)),
        out_shape=jax.ShapeDtypeStruct((A_HEADS, A_WIN, A_TQ), F32),
        name="mixer_a_bias",
    )(row)


def _mixer_b_kernel(q_ref, k_ref, vt_ref, o_ref, *scratch):
    s_refs = scratch[0:B_NB]
    mx_refs = scratch[B_NB:2 * B_NB]
    m_ref, acc_ref = scratch[2 * B_NB:]
    qi = pl.program_id(2)
    m_ref[...] = jnp.full(m_ref.shape, NEG_INF, F32)
    acc_ref[...] = jnp.zeros(acc_ref.shape, F32)

    def scores(blk, slot, lo=0):
        start = pl.multiple_of(blk * B_TK, B_TK)
        k = k_ref[0, pl.ds(start, B_TK), :]
        q = q_ref[0, lo:, :]
        s = lax.dot_general(k, q, (((1,), (1,)), ((), ())), preferred_element_type=F32)
        s_refs[slot][:, lo:] = s
        mx_refs[slot][:, lo:] = jnp.max(s, axis=0, keepdims=True)

    def accumulate(blk, s, mx, lo=0):
        start = pl.multiple_of(blk * B_TK, B_TK)
        m_old = m_ref[:, lo:]
        m_new = jnp.maximum(m_old, mx)
        alpha = jnp.exp2(m_old - m_new)
        p = jnp.exp2(s - m_new).astype(BF16)
        vt = vt_ref[0, :, pl.ds(start, B_TK)]
        acc_ref[:, lo:] = alpha * acc_ref[:, lo:] + jnp.dot(vt, p, preferred_element_type=F32)
        m_ref[:, lo:] = m_new

    def diagonal_mask(s):
        kc = lax.broadcasted_iota(jnp.int32, s.shape, 0) // CHUNK
        qc = lax.broadcasted_iota(jnp.int32, s.shape, 1) // CHUNK
        return jnp.where(kc <= qc, s, NEG_INF)

    for u in range(B_AHEAD):
        scores(u, u)

    def body(t, carry):
        for u in range(B_NB):
            scores(t * B_NB + u + B_AHEAD, (u + B_AHEAD) % B_NB)
            accumulate(t * B_NB + u, s_refs[u][...], mx_refs[u][...])
        return carry

    lax.fori_loop(0, qi, body, 0)
    for u in range(B_NB):
        lo = u * B_TK
        if u + B_AHEAD < B_NB:
            scores(qi * B_NB + u + B_AHEAD, u + B_AHEAD, lo=(u + B_AHEAD) * B_TK)
        s = diagonal_mask(s_refs[u][:, lo:])
        accumulate(qi * B_NB + u, s, jnp.max(s, axis=0, keepdims=True), lo=lo)
    denom = acc_ref[B_V_DIM:B_V_DIM + 1, :]
    o_ref[0] = (acc_ref[0:B_V_DIM, :] * (1.0 / denom)).astype(BF16)


def _mixer_b(qb, kb, vbt):
    B, S, _ = qb.shape
    return pl.pallas_call(
        _mixer_b_kernel,
        grid=(B, B_HEADS, S // B_TQ),
        in_specs=[pl.BlockSpec((1, B_TQ, LANES), lambda b, h, i: (b, i, h)),
                  pl.BlockSpec((1, S, LANES), lambda b, h, i: (b, 0, h)),
                  pl.BlockSpec((1, VT_ROWS, S), lambda b, h, i: (b, h, 0))],
        out_specs=pl.BlockSpec((1, B_V_DIM, B_TQ), lambda b, h, i: (b, h, i)),
        out_shape=jax.ShapeDtypeStruct((B, B_WIDTH, S), BF16),
        scratch_shapes=[pltpu.VMEM((B_TK, B_TQ), F32)] * B_NB + [pltpu.VMEM((1, B_TQ), F32)] * B_NB
                       + [pltpu.VMEM((1, B_TQ), F32), pltpu.VMEM((VT_ROWS, B_TQ), F32)],
        compiler_params=pltpu.CompilerParams(dimension_semantics=("arbitrary", "arbitrary", "arbitrary"),
                                             vmem_limit_bytes=VMEM_LIMIT),
        name="mixer_b",
    )(qb, kb, vbt)


def _merge_kernel(x_ref, yat_ref, ybt_ref, lng_ref, lnb_ref, wzg_ref, bzg_ref, wpa_ref, wpb_ref, wout_ref,
                  pg_ref, pb_ref, o_ref):
    h = _layer_norm(x_ref[0], lng_ref[...], lnb_ref[...])
    zg = jnp.dot(h.astype(BF16), wzg_ref[...], preferred_element_type=F32) + bzg_ref[...]
    za = zg[:, 0:A_WIDTH]
    zb = zg[:, A_WIDTH:A_WIDTH + B_WIDTH]
    ga = zg[:, A_WIDTH + B_WIDTH:A_WIDTH + B_WIDTH + D_MODEL]
    gb = zg[:, A_WIDTH + B_WIDTH + D_MODEL:]
    ya_in = yat_ref[0].astype(F32).T * (za * jax.nn.sigmoid(za))
    yb_in = ybt_ref[0].astype(F32).T * (zb * jax.nn.sigmoid(zb))
    ya = jnp.dot(ya_in.astype(BF16), wpa_ref[...], preferred_element_type=F32)
    yb = jnp.dot(yb_in.astype(BF16), wpb_ref[...], preferred_element_type=F32)
    mixed = jax.nn.sigmoid(ga) * ya + jax.nn.sigmoid(gb) * yb
    out = jnp.dot(mixed.astype(BF16), wout_ref[...], preferred_element_type=F32)
    o_ref[0] = _layer_norm(DEEPNORM_ALPHA * h + out, pg_ref[...], pb_ref[...])


def _merge(x, yat, ybt, lng, lnb, wzg, bzg, wpa, wpb, wout, pg, pb):
    B, S, _ = x.shape
    tm = OUT_TM
    row = lambda w: pl.BlockSpec((1, tm, w), lambda b, i: (b, i, 0))
    nzg = wzg.shape[1]
    return pl.pallas_call(
        _merge_kernel,
        grid=(B, S // tm),
        in_specs=[row(D_MODEL), pl.BlockSpec((1, A_WIDTH, tm), lambda b, i: (b, 0, i)),
                  pl.BlockSpec((1, B_WIDTH, tm), lambda b, i: (b, 0, i)),
                  _const_spec((1, D_MODEL)), _const_spec((1, D_MODEL)),
                  _const_spec((D_MODEL, nzg)), _const_spec((1, nzg)),
                  _const_spec((A_WIDTH, D_MODEL)), _const_spec((B_WIDTH, D_MODEL)),
                  _const_spec((D_MODEL, D_MODEL)), _const_spec((1, D_MODEL)), _const_spec((1, D_MODEL))],
        out_specs=row(D_MODEL),
        out_shape=jax.ShapeDtypeStruct((B, S, D_MODEL), F32),
        compiler_params=pltpu.CompilerParams(dimension_semantics=("arbitrary", "arbitrary"),
                                             vmem_limit_bytes=VMEM_LIMIT),
        name="merge_out",
    )(x, yat, ybt, lng, lnb, wzg, bzg, wpa, wpb, wout, pg, pb)


def _rot_cols(w):
    half = w.shape[-1] // 2
    return jnp.concatenate([-w[..., half:], w[..., :half]], axis=-1)


def _prep_layer(w_in, b_in, w_uq, w_ukv):
    c = 0
    cols = {}
    for name, width in (("aq", A_WIDTH), ("ak", A_WIDTH), ("av", A_WIDTH), ("az", A_WIDTH), ("cq", Q_LORA),
                        ("ckv", KV_LORA), ("kr", B_ROPE_DIM), ("bz", B_WIDTH), ("ga", D_MODEL), ("gb", D_MODEL)):
        cols[name] = slice(c, c + width)
        c += width
    w = lambda n: w_in[:, cols[n]]
    b = lambda n: b_in[cols[n]]
    a_scale = A_HEAD_DIM ** -0.5 * LOG2E
    zeros_w = jnp.zeros((D_MODEL, B_NOPE_DIM), F32)
    zeros_b = jnp.zeros((B_NOPE_DIM,), F32)
    w1 = jnp.concatenate([w("aq") * a_scale, w("ak"), w("cq"), w("ckv"),
                          zeros_w, w("kr"), w("kr"),
                          zeros_w, _rot_cols(w("kr")), _rot_cols(w("kr"))], axis=1)
    b1 = jnp.concatenate([b("aq") * a_scale, b("ak"), b("cq"), b("ckv"),
                          zeros_b, b("kr"), b("kr"),
                          zeros_b, _rot_cols(b("kr")), _rot_cols(b("kr"))])
    wzg = jnp.concatenate([w("az"), w("bz"), w("ga"), w("gb")], axis=1)
    bzg = jnp.concatenate([b("az"), b("bz"), b("ga"), b("gb")])

    uq = w_uq.reshape(Q_LORA, B_HEADS, B_QK_DIM)
    uq_rope = uq[:, :, B_NOPE_DIM:]
    wq = jnp.concatenate([uq, _rot_cols(uq_rope)], axis=-1).reshape(Q_LORA, B_HEADS * LANES)
    ukv = w_ukv.reshape(KV_LORA, B_HEADS, B_NOPE_DIM + B_V_DIM)
    wk = jnp.concatenate([ukv[:, :, :B_NOPE_DIM], jnp.zeros((KV_LORA, B_HEADS, LANES - B_NOPE_DIM), F32)],
                         axis=-1).reshape(KV_LORA, B_HEADS * LANES)
    wvt = ukv[:, :, B_NOPE_DIM:].reshape(KV_LORA, B_WIDTH).T
    return (w1.astype(BF16), b1[None, :], w("av").T.astype(BF16), b("av")[:, None], wq.astype(BF16),
            wk.astype(BF16), wvt.astype(BF16), wzg.astype(BF16), bzg[None, :])


def _rope_freq_row():
    half = B_ROPE_DIM // 2
    inv_freq = ROPE_THETA ** (-jnp.arange(half, dtype=F32) / half)
    return jnp.concatenate([jnp.zeros((B_NOPE_DIM,), F32), inv_freq, inv_freq, inv_freq, inv_freq])[None, :]


def kernel(x, positions, ln_in_g, ln_in_b, w_in, b_in, q_norm_g, kv_norm_g, w_uq, w_ukv, rel_bias, w_proj_a,
           w_proj_b, w_out, ln_post_g, ln_post_b):
    depth = w_in.shape[0]
    assert depth == 1, "the trunk-entry norm is recomputed per kernel, which is only valid for one layer"
    B, S, _ = x.shape
    pos3 = positions.reshape(B, S, 1)
    lng, lnb = ln_in_g[None, :], ln_in_b[None, :]
    freq = _rope_freq_row()
    l = 0
    w1, b1, wavt, bav, wq, wk, wvt, wzg, bzg = _prep_layer(w_in[l], b_in[l], w_uq[l], w_ukv[l])
    aq, ak, avt, qb, kb, vbt = _token_projections(x, pos3, lng, lnb, w1, b1, wavt, bav, q_norm_g[l][None, :],
                                                  kv_norm_g[l][None, :], wq, wk, wvt, freq)
    yat = _mixer_a(aq, ak, avt, _mixer_a_bias(rel_bias[l]))
    ybt = _mixer_b(qb, kb, vbt)
    return _merge(x, yat, ybt, lng, lnb, wzg, bzg, w_proj_a[l].astype(BF16), w_proj_b[l].astype(BF16),
                  w_out[l].astype(BF16), ln_post_g[l][None, :], ln_post_b[l][None, :])
```

```python
import jax
import jax.numpy as jnp
from jax import lax
from jax.experimental import pallas as pl
from jax.experimental.pallas import tpu as pltpu

D_MODEL = 1024
CHUNK = 64
A_HEADS = 8
A_HEAD_DIM = 64
A_WIDTH = A_HEADS * A_HEAD_DIM
A_LEFT_CHUNKS = 8
REL_CLIP = 128
B_HEADS = 8
B_NOPE_DIM = 64
B_ROPE_DIM = 32
B_QK_DIM = B_NOPE_DIM + B_ROPE_DIM
B_V_DIM = 64
B_WIDTH = B_HEADS * B_V_DIM
Q_LORA = 256
KV_LORA = 128
ROPE_THETA = 10000.0
DEEPNORM_ALPHA = 2.0 ** 0.25
LN_EPS = 1e-5
RMS_EPS = 1e-6
NEG_INF = -1e30

LANES = 128
VMEM_LIMIT = 56 * 1024 * 1024

PROJ_TM = 1024
A_TQ = 256
A_WIN = A_TQ + A_LEFT_CHUNKS * CHUNK
A_ROLL = 1024
B_TQ = 1024
B_TK = 256
B_NB = B_TQ // B_TK
B_AHEAD = 2
V_DIM = 64
VT_ROWS = 80
LOG2E = 1.4426950408889634
OUT_TM = 512

BF16 = jnp.bfloat16
F32 = jnp.float32


def _layer_norm(x, g, b):
    mu = jnp.mean(x, axis=-1, keepdims=True)
    xc = x - mu
    var = jnp.mean(xc * xc, axis=-1, keepdims=True)
    return xc * lax.rsqrt(var + LN_EPS) * g + b


def _rms_norm(x, g):
    return x * lax.rsqrt(jnp.mean(x * x, axis=-1, keepdims=True) + RMS_EPS) * g


def _const_spec(shape):
    nd = len(shape)
    return pl.BlockSpec(shape, lambda *_: (0,) * nd, pipeline_mode=pl.Buffered(1))


def _store_values_t(ref, vt):
    tail = VT_ROWS - V_DIM
    ones_rows = (lax.broadcasted_iota(jnp.int32, (tail, vt.shape[1]), 0) == 0).astype(BF16)
    for hd in range(vt.shape[0] // V_DIM):
        ref[0, hd * VT_ROWS:hd * VT_ROWS + V_DIM, :] = vt[hd * V_DIM:(hd + 1) * V_DIM].astype(BF16)
        ref[0, hd * VT_ROWS + V_DIM:(hd + 1) * VT_ROWS, :] = ones_rows


def _proj_kernel(x_ref, pos_ref, lng_ref, lnb_ref, w1_ref, b1_ref, wavt_ref, bav_ref, qg_ref, kvg_ref, wq_ref,
                 wk_ref, wvt_ref, freq_ref, aq_ref, ak_ref, avt_ref, qb_ref, kb_ref, vbt_ref):
    hb = _layer_norm(x_ref[0], lng_ref[...], lnb_ref[...]).astype(BF16)
    proj = jnp.dot(hb, w1_ref[...], preferred_element_type=F32) + b1_ref[...]
    aq_ref[0] = proj[:, 0:A_WIDTH].astype(BF16)
    ak_ref[0] = proj[:, A_WIDTH:2 * A_WIDTH].astype(BF16)
    avt = lax.dot_general(wavt_ref[...], hb, (((1,), (1,)), ((), ())), preferred_element_type=F32) + bav_ref[...]
    _store_values_t(avt_ref, avt)
    o = 2 * A_WIDTH
    cq = _rms_norm(proj[:, o:o + Q_LORA], qg_ref[...]).astype(BF16)
    o += Q_LORA
    ckv = _rms_norm(proj[:, o:o + KV_LORA], kvg_ref[...]).astype(BF16)
    o += KV_LORA
    kr = proj[:, o:o + LANES]
    kr_rot = proj[:, o + LANES:o + 2 * LANES]

    ang = pos_ref[0].astype(F32) * freq_ref[...]
    cos_t = jnp.cos(ang)
    sin_t = jnp.sin(ang)
    lane = lax.broadcasted_iota(jnp.int32, (1, LANES), 1)
    scale = B_QK_DIM ** -0.5 * LOG2E
    tq = jnp.where(lane < B_NOPE_DIM, scale, jnp.where(lane < B_QK_DIM, cos_t, sin_t) * scale)
    k_rope = kr * cos_t + kr_rot * sin_t

    q_all = jnp.dot(cq, wq_ref[...], preferred_element_type=F32)
    k_all = jnp.dot(ckv, wk_ref[...], preferred_element_type=F32)
    for hd in range(B_HEADS):
        sl = slice(hd * LANES, (hd + 1) * LANES)
        qb_ref[0, :, sl] = (q_all[:, sl] * tq).astype(BF16)
        kb_ref[0, :, sl] = (k_all[:, sl] + k_rope).astype(BF16)
    vt = lax.dot_general(wvt_ref[...], ckv, (((1,), (1,)), ((), ())), preferred_element_type=F32)
    _store_values_t(vbt_ref, vt)


def _token_projections(x, pos3, lng, lnb, w1, b1, wavt, bav, qg, kvg, wq, wk, wvt, freq):
    B, S, _ = x.shape
    tm = PROJ_TM
    n1 = w1.shape[1]
    row = lambda w: pl.BlockSpec((1, tm, w), lambda b, i: (b, i, 0))
    vt_spec = pl.BlockSpec((1, B_HEADS * VT_ROWS, tm), lambda b, i: (b, 0, i))
    vt_shape = jax.ShapeDtypeStruct((B, B_HEADS * VT_ROWS, S), BF16)
    return pl.pallas_call(
        _proj_kernel,
        grid=(B, S // tm),
        in_specs=[row(D_MODEL), row(1),
                  _const_spec((1, D_MODEL)), _const_spec((1, D_MODEL)),
                  _const_spec((D_MODEL, n1)), _const_spec((1, n1)),
                  _const_spec((A_WIDTH, D_MODEL)), _const_spec((A_WIDTH, 1)),
                  _const_spec((1, Q_LORA)), _const_spec((1, KV_LORA)),
                  _const_spec((Q_LORA, B_HEADS * LANES)), _const_spec((KV_LORA, B_HEADS * LANES)),
                  _const_spec((B_WIDTH, KV_LORA)), _const_spec((1, LANES))],
        out_specs=[row(A_WIDTH), row(A_WIDTH), vt_spec, row(B_HEADS * LANES), row(B_HEADS * LANES), vt_spec],
        out_shape=[jax.ShapeDtypeStruct((B, S, A_WIDTH), BF16)] * 2 + [vt_shape]
                  + [jax.ShapeDtypeStruct((B, S, B_HEADS * LANES), BF16)] * 2 + [vt_shape],
        compiler_params=pltpu.CompilerParams(dimension_semantics=("arbitrary", "arbitrary"),
                                             vmem_limit_bytes=VMEM_LIMIT),
        name="token_projections",
    )(x, pos3, lng, lnb, w1, b1, wavt, bav, qg, kvg, wq, wk, wvt, freq)


def _mixer_a_kernel(q_ref, k2_ref, k1_ref, k0_ref, v2_ref, v1_ref, v0_ref, bias_ref, o_ref,
                    s0_ref, s1_ref, mx0_ref, mx1_ref):
    i = pl.program_id(1)
    s_refs, mx_refs = (s0_ref, s1_ref), (mx0_ref, mx1_ref)
    n_old = A_LEFT_CHUNKS * CHUNK
    kpos = lax.broadcasted_iota(jnp.int32, (n_old, 1), 0) + (i * A_TQ - n_old)
    pad_mask = jnp.where(kpos >= 0, 0.0, NEG_INF).astype(F32)
    lane = lax.broadcasted_iota(jnp.int32, (1, LANES), 1)

    def scores(pair, slot):
        sl = slice(pair * LANES, (pair + 1) * LANES)
        qp = q_ref[0, :, sl]
        kp = jnp.concatenate([k2_ref[0, :, sl], k1_ref[0, :, sl], k0_ref[0, :, sl]], axis=0)
        q2 = jnp.concatenate([jnp.where((lane // A_HEAD_DIM) == hh, qp, jnp.zeros_like(qp)) for hh in range(2)],
                             axis=0)
        s = lax.dot_general(kp, q2, (((1,), (1,)), ((), ())), preferred_element_type=F32)
        s = s + jnp.concatenate([bias_ref[2 * pair], bias_ref[2 * pair + 1]], axis=1)
        s = jnp.concatenate([s[:n_old] + pad_mask, s[n_old:]], axis=0)
        s_refs[slot][...] = s
        mx_refs[slot][...] = jnp.max(s, axis=0, keepdims=True)

    def finish(pair, slot):
        p = jnp.exp2(s_refs[slot][...] - mx_refs[slot][...]).astype(BF16)
        rows = slice(2 * pair * VT_ROWS, (2 * pair + 2) * VT_ROWS)
        vt = jnp.concatenate([v2_ref[0, rows, :], v1_ref[0, rows, :], v0_ref[0, rows, :]], axis=1)
        o = jnp.dot(vt, p, preferred_element_type=F32)
        for hh in range(2):
            oh = o[hh * VT_ROWS:(hh + 1) * VT_ROWS, hh * A_TQ:(hh + 1) * A_TQ]
            hd = 2 * pair + hh
            o_ref[0, hd * V_DIM:(hd + 1) * V_DIM, :] = (oh[:V_DIM] * (1.0 / oh[V_DIM:V_DIM + 1])).astype(BF16)

    n_pairs = A_HEADS // 2
    scores(0, 0)
    for pair in range(n_pairs):
        if pair + 1 < n_pairs:
            scores(pair + 1, (pair + 1) % 2)
        finish(pair, pair % 2)


def _mixer_a(aq, ak, avt, bias):
    B, S, W = aq.shape
    tq = A_TQ
    blk = lambda back: pl.BlockSpec((1, tq, W), lambda b, i: (b, jnp.maximum(i - back, 0), 0))
    vblk = lambda back: pl.BlockSpec((1, A_HEADS * VT_ROWS, tq), lambda b, i: (b, 0, jnp.maximum(i - back, 0)))
    return pl.pallas_call(
        _mixer_a_kernel,
        grid=(B, S // tq),
        in_specs=[blk(0), blk(2), blk(1), blk(0), vblk(2), vblk(1), vblk(0), _const_spec(bias.shape)],
        out_specs=pl.BlockSpec((1, W, tq), lambda b, i: (b, 0, i)),
        out_shape=jax.ShapeDtypeStruct((B, W, S), BF16),
        scratch_shapes=[pltpu.VMEM((A_WIN, 2 * tq), F32)] * 2 + [pltpu.VMEM((1, 2 * tq), F32)] * 2,
        compiler_params=pltpu.CompilerParams(dimension_semantics=("arbitrary", "arbitrary"),
                                             vmem_limit_bytes=VMEM_LIMIT),
        name="mixer_a",
    )(aq, ak, ak, ak, avt, avt, avt, bias)


def _bias_kernel(row_ref, o_ref):
    rows = jnp.broadcast_to(row_ref[0], (A_WIN, A_ROLL))
    table = pltpu.roll(rows, 0, 1, stride=1, stride_axis=0)[:, :A_TQ]
    kc = lax.broadcasted_iota(jnp.int32, (A_WIN, A_TQ), 0) // CHUNK
    qc = lax.broadcasted_iota(jnp.int32, (A_WIN, A_TQ), 1) // CHUNK
    gap = qc + A_LEFT_CHUNKS - kc
    o_ref[0] = jnp.where((gap >= 0) & (gap <= A_LEFT_CHUNKS), table * LOG2E, NEG_INF)


def _mixer_a_bias(rel_bias):
    tbl = rel_bias.T.astype(F32)
    n_old = A_LEFT_CHUNKS * CHUNK
    first_tbl = A_ROLL - n_old - REL_CLIP
    far = jnp.broadcast_to(tbl[:, -1:], (A_HEADS, A_ROLL))
    near = jnp.broadcast_to(tbl[:, :1], (A_HEADS, A_ROLL))
    row = jnp.concatenate([far[:, :A_ROLL - A_WIN + 1], near[:, A_ROLL - A_WIN + 1:first_tbl], tbl,
                           far[:, first_tbl + tbl.shape[1]:]], axis=1)
    row = row[:, None, :]
    return pl.pallas_call(
        _bias_kernel,
        grid=(A_HEADS,),
        in_specs=[pl.BlockSpec((1, 1, A_ROLL), lambda h: (h, 0, 0))],
        out_specs=pl.BlockSpec((1, A_WIN, A_TQ), lambda h: (h, 0, 0)),
        out_shape=jax.ShapeDtypeStruct((A_HEADS, A_WIN, A_TQ), F32),
        name="mixer_a_bias",
    )(row)


def _mixer_b_kernel(q_ref, k_ref, vt_ref, o_ref, *scratch):
    s_refs = scratch[0:B_NB]
    mx_refs = scratch[B_NB:2 * B_NB]
    m_ref, acc_ref = scratch[2 * B_NB:]
    n_q = q_ref.shape[1] // B_TQ

    def scores(blk, slot, q0, lo=0):
        start = pl.multiple_of(blk * B_TK, B_TK)
        k = k_ref[0, pl.ds(start, B_TK), :]
        q = q_ref[0, pl.ds(pl.multiple_of(q0 + lo, B_TK), B_TQ - lo), :]
        s = lax.dot_general(k, q, (((1,), (1,)), ((), ())), preferred_element_type=F32)
        s_refs[slot][:, lo:] = s
        mx_refs[slot][:, lo:] = jnp.max(s, axis=0, keepdims=True)

    def accumulate(blk, s, mx, lo=0):
        start = pl.multiple_of(blk * B_TK, B_TK)
        m_old = m_ref[:, lo:]
        m_new = jnp.maximum(m_old, mx)
        alpha = jnp.exp2(m_old - m_new)
        p = jnp.exp2(s - m_new).astype(BF16)
        vt = vt_ref[0, :, pl.ds(start, B_TK)]
        acc_ref[:, lo:] = alpha * acc_ref[:, lo:] + jnp.dot(vt, p, preferred_element_type=F32)
        m_ref[:, lo:] = m_new

    def diagonal_mask(s):
        kc = lax.broadcasted_iota(jnp.int32, s.shape, 0) // CHUNK
        qc = lax.broadcasted_iota(jnp.int32, s.shape, 1) // CHUNK
        return jnp.where(kc <= qc, s, NEG_INF)

    for u in range(B_AHEAD):
        scores(u, u, 0)

    def query_tile(qi, carry):
        q0 = qi * B_TQ
        m_ref[...] = jnp.full(m_ref.shape, NEG_INF, F32)
        acc_ref[...] = jnp.zeros(acc_ref.shape, F32)

        def body(t, c):
            for u in range(B_NB):
                scores(t * B_NB + u + B_AHEAD, (u + B_AHEAD) % B_NB, q0)
                accumulate(t * B_NB + u, s_refs[u][...], mx_refs[u][...])
            return c

        lax.fori_loop(0, qi, body, 0)
        q0_next = jnp.minimum(qi + 1, n_q - 1) * B_TQ
        for u in range(B_NB):
            lo = u * B_TK
            if u + B_AHEAD < B_NB:
                scores(qi * B_NB + u + B_AHEAD, u + B_AHEAD, q0, lo=(u + B_AHEAD) * B_TK)
            else:
                scores(u + B_AHEAD - B_NB, u + B_AHEAD - B_NB, q0_next)
            s = diagonal_mask(s_refs[u][:, lo:])
            accumulate(qi * B_NB + u, s, jnp.max(s, axis=0, keepdims=True), lo=lo)
        denom = acc_ref[B_V_DIM:B_V_DIM + 1, :]
        o_ref[0, :, pl.ds(pl.multiple_of(q0, B_TQ), B_TQ)] = (acc_ref[0:B_V_DIM, :] * (1.0 / denom)).astype(BF16)
        return carry

    lax.fori_loop(0, n_q, query_tile, 0)


def _mixer_b(qb, kb, vbt):
    B, S, _ = qb.shape
    return pl.pallas_call(
        _mixer_b_kernel,
        grid=(B, B_HEADS),
        in_specs=[pl.BlockSpec((1, S, LANES), lambda b, h: (b, 0, h)),
                  pl.BlockSpec((1, S, LANES), lambda b, h: (b, 0, h)),
                  pl.BlockSpec((1, VT_ROWS, S), lambda b, h: (b, h, 0))],
        out_specs=pl.BlockSpec((1, B_V_DIM, S), lambda b, h: (b, h, 0)),
        out_shape=jax.ShapeDtypeStruct((B, B_WIDTH, S), BF16),
        scratch_shapes=[pltpu.VMEM((B_TK, B_TQ), F32)] * B_NB + [pltpu.VMEM((1, B_TQ), F32)] * B_NB
                       + [pltpu.VMEM((1, B_TQ), F32), pltpu.VMEM((VT_ROWS, B_TQ), F32)],
        compiler_params=pltpu.CompilerParams(dimension_semantics=("arbitrary", "arbitrary"),
                                             vmem_limit_bytes=VMEM_LIMIT),
        name="mixer_b",
    )(qb, kb, vbt)


def _merge_kernel(x_ref, yat_ref, ybt_ref, lng_ref, lnb_ref, wzg_ref, bzg_ref, wpa_ref, wpb_ref, wout_ref,
                  pg_ref, pb_ref, o_ref):
    h = _layer_norm(x_ref[0], lng_ref[...], lnb_ref[...])
    zg = jnp.dot(h.astype(BF16), wzg_ref[...], preferred_element_type=F32) + bzg_ref[...]
    za = zg[:, 0:A_WIDTH]
    zb = zg[:, A_WIDTH:A_WIDTH + B_WIDTH]
    ga = zg[:, A_WIDTH + B_WIDTH:A_WIDTH + B_WIDTH + D_MODEL]
    gb = zg[:, A_WIDTH + B_WIDTH + D_MODEL:]
    ya_in = yat_ref[0].astype(F32).T * (za * jax.nn.sigmoid(za))
    yb_in = ybt_ref[0].astype(F32).T * (zb * jax.nn.sigmoid(zb))
    ya = jnp.dot(ya_in.astype(BF16), wpa_ref[...], preferred_element_type=F32)
    yb = jnp.dot(yb_in.astype(BF16), wpb_ref[...], preferred_element_type=F32)
    mixed = jax.nn.sigmoid(ga) * ya + jax.nn.sigmoid(gb) * yb
    out = jnp.dot(mixed.astype(BF16), wout_ref[...], preferred_element_type=F32)
    o_ref[0] = _layer_norm(DEEPNORM_ALPHA * h + out, pg_ref[...], pb_ref[...])


def _merge(x, yat, ybt, lng, lnb, wzg, bzg, wpa, wpb, wout, pg, pb):
    B, S, _ = x.shape
    tm = OUT_TM
    row = lambda w: pl.BlockSpec((1, tm, w), lambda b, i: (b, i, 0))
    nzg = wzg.shape[1]
    return pl.pallas_call(
        _merge_kernel,
        grid=(B, S // tm),
        in_specs=[row(D_MODEL), pl.BlockSpec((1, A_WIDTH, tm), lambda b, i: (b, 0, i)),
                  pl.BlockSpec((1, B_WIDTH, tm), lambda b, i: (b, 0, i)),
                  _const_spec((1, D_MODEL)), _const_spec((1, D_MODEL)),
                  _const_spec((D_MODEL, nzg)), _const_spec((1, nzg)),
                  _const_spec((A_WIDTH, D_MODEL)), _const_spec((B_WIDTH, D_MODEL)),
                  _const_spec((D_MODEL, D_MODEL)), _const_spec((1, D_MODEL)), _const_spec((1, D_MODEL))],
        out_specs=row(D_MODEL),
        out_shape=jax.ShapeDtypeStruct((B, S, D_MODEL), F32),
        compiler_params=pltpu.CompilerParams(dimension_semantics=("arbitrary", "arbitrary"),
                                             vmem_limit_bytes=VMEM_LIMIT),
        name="merge_out",
    )(x, yat, ybt, lng, lnb, wzg, bzg, wpa, wpb, wout, pg, pb)


def _rot_cols(w):
    half = w.shape[-1] // 2
    return jnp.concatenate([-w[..., half:], w[..., :half]], axis=-1)


def _prep_layer(w_in, b_in, w_uq, w_ukv):
    c = 0
    cols = {}
    for name, width in (("aq", A_WIDTH), ("ak", A_WIDTH), ("av", A_WIDTH), ("az", A_WIDTH), ("cq", Q_LORA),
                        ("ckv", KV_LORA), ("kr", B_ROPE_DIM), ("bz", B_WIDTH), ("ga", D_MODEL), ("gb", D_MODEL)):
        cols[name] = slice(c, c + width)
        c += width
    w = lambda n: w_in[:, cols[n]]
    b = lambda n: b_in[cols[n]]
    a_scale = A_HEAD_DIM ** -0.5 * LOG2E
    zeros_w = jnp.zeros((D_MODEL, B_NOPE_DIM), F32)
    zeros_b = jnp.zeros((B_NOPE_DIM,), F32)
    w1 = jnp.concatenate([w("aq") * a_scale, w("ak"), w("cq"), w("ckv"),
                          zeros_w, w("kr"), w("kr"),
                          zeros_w, _rot_cols(w("kr")), _rot_cols(w("kr"))], axis=1)
    b1 = jnp.concatenate([b("aq") * a_scale, b("ak"), b("cq"), b("ckv"),
                          zeros_b, b("kr"), b("kr"),
                          zeros_b, _rot_cols(b("kr")), _rot_cols(b("kr"))])
    wzg = jnp.concatenate([w("az"), w("bz"), w("ga"), w("gb")], axis=1)
    bzg = jnp.concatenate([b("az"), b("bz"), b("ga"), b("gb")])

    uq = w_uq.reshape(Q_LORA, B_HEADS, B_QK_DIM)
    uq_rope = uq[:, :, B_NOPE_DIM:]
    wq = jnp.concatenate([uq, _rot_cols(uq_rope)], axis=-1).reshape(Q_LORA, B_HEADS * LANES)
    ukv = w_ukv.reshape(KV_LORA, B_HEADS, B_NOPE_DIM + B_V_DIM)
    wk = jnp.concatenate([ukv[:, :, :B_NOPE_DIM], jnp.zeros((KV_LORA, B_HEADS, LANES - B_NOPE_DIM), F32)],
                         axis=-1).reshape(KV_LORA, B_HEADS * LANES)
    wvt = ukv[:, :, B_NOPE_DIM:].reshape(KV_LORA, B_WIDTH).T
    return (w1.astype(BF16), b1[None, :], w("av").T.astype(BF16), b("av")[:, None], wq.astype(BF16),
            wk.astype(BF16), wvt.astype(BF16), wzg.astype(BF16), bzg[None, :])


def _rope_freq_row():
    half = B_ROPE_DIM // 2
    inv_freq = ROPE_THETA ** (-jnp.arange(half, dtype=F32) / half)
    return jnp.concatenate([jnp.zeros((B_NOPE_DIM,), F32), inv_freq, inv_freq, inv_freq, inv_freq])[None, :]


def kernel(x, positions, ln_in_g, ln_in_b, w_in, b_in, q_norm_g, kv_norm_g, w_uq, w_ukv, rel_bias, w_proj_a,
           w_proj_b, w_out, ln_post_g, ln_post_b):
    depth = w_in.shape[0]
    assert depth == 1, "the trunk-entry norm is recomputed per kernel, which is only valid for one layer"
    B, S, _ = x.shape
    pos3 = positions.reshape(B, S, 1)
    lng, lnb = ln_in_g[None, :], ln_in_b[None, :]
    freq = _rope_freq_row()
    l = 0
    w1, b1, wavt, bav, wq, wk, wvt, wzg, bzg = _prep_layer(w_in[l], b_in[l], w_uq[l], w_ukv[l])
    aq, ak, avt, qb, kb, vbt = _token_projections(x, pos3, lng, lnb, w1, b1, wavt, bav, q_norm_g[l][None, :],
                                                  kv_norm_g[l][None, :], wq, wk, wvt, freq)
    yat = _mixer_a(aq, ak, avt, _mixer_a_bias(rel_bias[l]))
    ybt = _mixer_b(qb, kb, vbt)
    return _merge(x, yat, ybt, lng, lnb, wzg, bzg, w_proj_a[l].astype(BF16), w_proj_b[l].astype(BF16),
                  w_out[l].astype(BF16), ln_post_g[l][None, :], ln_post_b[l][None, :])
```

```python
import jax
import jax.numpy as jnp
from jax import lax
from jax.experimental import pallas as pl
from jax.experimental.pallas import tpu as pltpu

D_MODEL = 1024
CHUNK = 64
A_HEADS = 8
A_HEAD_DIM = 64
A_WIDTH = A_HEADS * A_HEAD_DIM
A_LEFT_CHUNKS = 8
REL_CLIP = 128
B_HEADS = 8
B_NOPE_DIM = 64
B_ROPE_DIM = 32
B_QK_DIM = B_NOPE_DIM + B_ROPE_DIM
B_V_DIM = 64
B_WIDTH = B_HEADS * B_V_DIM
Q_LORA = 256
KV_LORA = 128
ROPE_THETA = 10000.0
DEEPNORM_ALPHA = 2.0 ** 0.25
LN_EPS = 1e-5
RMS_EPS = 1e-6
NEG_INF = -1e30

LANES = 128
VMEM_LIMIT = 56 * 1024 * 1024

PROJ_TM = 1024
A_TQ = 256
A_WIN = A_TQ + A_LEFT_CHUNKS * CHUNK
A_ROLL = 1024
B_TQ = 1024
B_TK = 256
B_NB = B_TQ // B_TK
B_AHEAD = 2
V_DIM = 64
VT_ROWS = 80
LOG2E = 1.4426950408889634
OUT_TM = 512
MERGE_GROUPS = 2

BF16 = jnp.bfloat16
F32 = jnp.float32


def _layer_norm(x, g, b):
    mu = jnp.mean(x, axis=-1, keepdims=True)
    xc = x - mu
    var = jnp.mean(xc * xc, axis=-1, keepdims=True)
    return xc * lax.rsqrt(var + LN_EPS) * g + b


def _rms_norm(x, g):
    return x * lax.rsqrt(jnp.mean(x * x, axis=-1, keepdims=True) + RMS_EPS) * g


def _const_spec(shape):
    nd = len(shape)
    return pl.BlockSpec(shape, lambda *_: (0,) * nd, pipeline_mode=pl.Buffered(1))


def _store_values_t(ref, vt):
    tail = VT_ROWS - V_DIM
    ones_rows = (lax.broadcasted_iota(jnp.int32, (tail, vt.shape[1]), 0) == 0).astype(BF16)
    for hd in range(vt.shape[0] // V_DIM):
        ref[0, hd * VT_ROWS:hd * VT_ROWS + V_DIM, :] = vt[hd * V_DIM:(hd + 1) * V_DIM].astype(BF16)
        ref[0, hd * VT_ROWS + V_DIM:(hd + 1) * VT_ROWS, :] = ones_rows


def _proj_kernel(x_ref, pos_ref, lng_ref, lnb_ref, w1_ref, b1_ref, wavt_ref, bav_ref, qg_ref, kvg_ref, wq_ref,
                 wk_ref, wvt_ref, freq_ref, aq_ref, ak_ref, avt_ref, qb_ref, kb_ref, vbt_ref):
    hb = _layer_norm(x_ref[0], lng_ref[...], lnb_ref[...]).astype(BF16)
    proj = jnp.dot(hb, w1_ref[...], preferred_element_type=F32) + b1_ref[...]
    aq_ref[0] = proj[:, 0:A_WIDTH].astype(BF16)
    ak_ref[0] = proj[:, A_WIDTH:2 * A_WIDTH].astype(BF16)
    avt = lax.dot_general(wavt_ref[...], hb, (((1,), (1,)), ((), ())), preferred_element_type=F32) + bav_ref[...]
    _store_values_t(avt_ref, avt)
    o = 2 * A_WIDTH
    cq = _rms_norm(proj[:, o:o + Q_LORA], qg_ref[...]).astype(BF16)
    o += Q_LORA
    ckv = _rms_norm(proj[:, o:o + KV_LORA], kvg_ref[...]).astype(BF16)
    o += KV_LORA
    kr = proj[:, o:o + LANES]
    kr_rot = proj[:, o + LANES:o + 2 * LANES]

    pos_rows = jnp.broadcast_to(pos_ref[0].astype(F32), (LANES, pos_ref.shape[2])).T
    ang = pos_rows * freq_ref[...]
    cos_t = jnp.cos(ang)
    sin_t = jnp.sin(ang)
    lane = lax.broadcasted_iota(jnp.int32, (1, LANES), 1)
    scale = B_QK_DIM ** -0.5 * LOG2E
    tq = jnp.where(lane < B_NOPE_DIM, scale, jnp.where(lane < B_QK_DIM, cos_t, sin_t) * scale)
    k_rope = kr * cos_t + kr_rot * sin_t

    q_all = jnp.dot(cq, wq_ref[...], preferred_element_type=F32)
    k_all = jnp.dot(ckv, wk_ref[...], preferred_element_type=F32)
    for hd in range(B_HEADS):
        sl = slice(hd * LANES, (hd + 1) * LANES)
        qb_ref[0, :, sl] = (q_all[:, sl] * tq).astype(BF16)
        kb_ref[0, :, sl] = (k_all[:, sl] + k_rope).astype(BF16)
    vt = lax.dot_general(wvt_ref[...], ckv, (((1,), (1,)), ((), ())), preferred_element_type=F32)
    _store_values_t(vbt_ref, vt)


def _token_projections(x, pos3, lng, lnb, w1, b1, wavt, bav, qg, kvg, wq, wk, wvt, freq):
    B, S, _ = x.shape
    tm = PROJ_TM
    n1 = w1.shape[1]
    row = lambda w: pl.BlockSpec((1, tm, w), lambda b, i: (b, i, 0))
    vt_spec = pl.BlockSpec((1, B_HEADS * VT_ROWS, tm), lambda b, i: (b, 0, i))
    vt_shape = jax.ShapeDtypeStruct((B, B_HEADS * VT_ROWS, S), BF16)
    return pl.pallas_call(
        _proj_kernel,
        grid=(B, S // tm),
        in_specs=[row(D_MODEL), pl.BlockSpec((1, 1, tm), lambda b, i: (b, 0, i)),
                  _const_spec((1, D_MODEL)), _const_spec((1, D_MODEL)),
                  _const_spec((D_MODEL, n1)), _const_spec((1, n1)),
                  _const_spec((A_WIDTH, D_MODEL)), _const_spec((A_WIDTH, 1)),
                  _const_spec((1, Q_LORA)), _const_spec((1, KV_LORA)),
                  _const_spec((Q_LORA, B_HEADS * LANES)), _const_spec((KV_LORA, B_HEADS * LANES)),
                  _const_spec((B_WIDTH, KV_LORA)), _const_spec((1, LANES))],
        out_specs=[row(A_WIDTH), row(A_WIDTH), vt_spec, row(B_HEADS * LANES), row(B_HEADS * LANES), vt_spec],
        out_shape=[jax.ShapeDtypeStruct((B, S, A_WIDTH), BF16)] * 2 + [vt_shape]
                  + [jax.ShapeDtypeStruct((B, S, B_HEADS * LANES), BF16)] * 2 + [vt_shape],
        compiler_params=pltpu.CompilerParams(dimension_semantics=("arbitrary", "arbitrary"),
                                             vmem_limit_bytes=VMEM_LIMIT),
        name="token_projections",
    )(x, pos3, lng, lnb, w1, b1, wavt, bav, qg, kvg, wq, wk, wvt, freq)


def _mixer_a_kernel(q_ref, k2_ref, k1_ref, k0_ref, v2_ref, v1_ref, v0_ref, bias_ref, o_ref,
                    s0_ref, s1_ref, mx0_ref, mx1_ref):
    i = pl.program_id(1)
    s_refs, mx_refs = (s0_ref, s1_ref), (mx0_ref, mx1_ref)
    n_old = A_LEFT_CHUNKS * CHUNK
    kpos = lax.broadcasted_iota(jnp.int32, (n_old, 1), 0) + (i * A_TQ - n_old)
    pad_mask = jnp.where(kpos >= 0, 0.0, NEG_INF).astype(F32)
    lane = lax.broadcasted_iota(jnp.int32, (1, LANES), 1)

    def scores(pair, slot):
        sl = slice(pair * LANES, (pair + 1) * LANES)
        qp = q_ref[0, :, sl]
        kp = jnp.concatenate([k2_ref[0, :, sl], k1_ref[0, :, sl], k0_ref[0, :, sl]], axis=0)
        q2 = jnp.concatenate([jnp.where((lane // A_HEAD_DIM) == hh, qp, jnp.zeros_like(qp)) for hh in range(2)],
                             axis=0)
        s = lax.dot_general(kp, q2, (((1,), (1,)), ((), ())), preferred_element_type=F32)
        s = s + jnp.concatenate([bias_ref[2 * pair], bias_ref[2 * pair + 1]], axis=1)
        s = jnp.concatenate([s[:n_old] + pad_mask, s[n_old:]], axis=0)
        s_refs[slot][...] = s
        mx_refs[slot][...] = jnp.max(s, axis=0, keepdims=True)

    def finish(pair, slot):
        p = jnp.exp2(s_refs[slot][...] - mx_refs[slot][...]).astype(BF16)
        rows = slice(2 * pair * VT_ROWS, (2 * pair + 2) * VT_ROWS)
        vt = jnp.concatenate([v2_ref[0, rows, :], v1_ref[0, rows, :], v0_ref[0, rows, :]], axis=1)
        for hh in range(2):
            oh = jnp.dot(vt[hh * VT_ROWS:(hh + 1) * VT_ROWS], p[:, hh * A_TQ:(hh + 1) * A_TQ],
                         preferred_element_type=F32)
            hd = 2 * pair + hh
            o_ref[0, hd * V_DIM:(hd + 1) * V_DIM, :] = (oh[:V_DIM] * (1.0 / oh[V_DIM:V_DIM + 1])).astype(BF16)

    n_pairs = A_HEADS // 2
    scores(0, 0)
    for pair in range(n_pairs):
        if pair + 1 < n_pairs:
            scores(pair + 1, (pair + 1) % 2)
        finish(pair, pair % 2)


def _mixer_a(aq, ak, avt, bias):
    B, S, W = aq.shape
    tq = A_TQ
    blk = lambda back: pl.BlockSpec((1, tq, W), lambda b, i: (b, jnp.maximum(i - back, 0), 0))
    vblk = lambda back: pl.BlockSpec((1, A_HEADS * VT_ROWS, tq), lambda b, i: (b, 0, jnp.maximum(i - back, 0)))
    return pl.pallas_call(
        _mixer_a_kernel,
        grid=(B, S // tq),
        in_specs=[blk(0), blk(2), blk(1), blk(0), vblk(2), vblk(1), vblk(0), _const_spec(bias.shape)],
        out_specs=pl.BlockSpec((1, W, tq), lambda b, i: (b, 0, i)),
        out_shape=jax.ShapeDtypeStruct((B, W, S), BF16),
        scratch_shapes=[pltpu.VMEM((A_WIN, 2 * tq), F32)] * 2 + [pltpu.VMEM((1, 2 * tq), F32)] * 2,
        compiler_params=pltpu.CompilerParams(dimension_semantics=("arbitrary", "arbitrary"),
                                             vmem_limit_bytes=VMEM_LIMIT),
        name="mixer_a",
    )(aq, ak, ak, ak, avt, avt, avt, bias)


def _bias_kernel(row_ref, o_ref):
    rows = jnp.broadcast_to(row_ref[0], (A_WIN, A_ROLL))
    table = pltpu.roll(rows, 0, 1, stride=1, stride_axis=0)[:, :A_TQ]
    kc = lax.broadcasted_iota(jnp.int32, (A_WIN, A_TQ), 0) // CHUNK
    qc = lax.broadcasted_iota(jnp.int32, (A_WIN, A_TQ), 1) // CHUNK
    gap = qc + A_LEFT_CHUNKS - kc
    o_ref[0] = jnp.where((gap >= 0) & (gap <= A_LEFT_CHUNKS), table * LOG2E, NEG_INF)


def _mixer_a_bias(rel_bias):
    tbl = rel_bias.T.astype(F32)
    n_old = A_LEFT_CHUNKS * CHUNK
    first_tbl = A_ROLL - n_old - REL_CLIP
    far = jnp.broadcast_to(tbl[:, -1:], (A_HEADS, A_ROLL))
    near = jnp.broadcast_to(tbl[:, :1], (A_HEADS, A_ROLL))
    row = jnp.concatenate([far[:, :A_ROLL - A_WIN + 1], near[:, A_ROLL - A_WIN + 1:first_tbl], tbl,
                           far[:, first_tbl + tbl.shape[1]:]], axis=1)
    row = row[:, None, :]
    return pl.pallas_call(
        _bias_kernel,
        grid=(A_HEADS,),
        in_specs=[pl.BlockSpec((1, 1, A_ROLL), lambda h: (h, 0, 0))],
        out_specs=pl.BlockSpec((1, A_WIN, A_TQ), lambda h: (h, 0, 0)),
        out_shape=jax.ShapeDtypeStruct((A_HEADS, A_WIN, A_TQ), F32),
        name="mixer_a_bias",
    )(row)


def _mixer_b_kernel(q_ref, k_ref, vt_ref, o_ref, *scratch):
    s_refs = scratch[0:B_NB]
    mx_refs = scratch[B_NB:2 * B_NB]
    m_ref, acc_ref = scratch[2 * B_NB:]
    n_q = q_ref.shape[1] // B_TQ

    def scores(blk, slot, q0, lo=0):
        start = pl.multiple_of(blk * B_TK, B_TK)
        k = k_ref[0, pl.ds(start, B_TK), :]
        q = q_ref[0, pl.ds(pl.multiple_of(q0 + lo, B_TK), B_TQ - lo), :]
        s = lax.dot_general(k, q, (((1,), (1,)), ((), ())), preferred_element_type=F32)
        s_refs[slot][:, lo:] = s
        mx_refs[slot][:, lo:] = jnp.max(s, axis=0, keepdims=True)

    def accumulate(blk, s, mx, lo=0):
        start = pl.multiple_of(blk * B_TK, B_TK)
        m_old = m_ref[:, lo:]
        m_new = jnp.maximum(m_old, mx)
        alpha = jnp.exp2(m_old - m_new)
        p = jnp.exp2(s - m_new).astype(BF16)
        vt = vt_ref[0, :, pl.ds(start, B_TK)]
        acc_ref[:, lo:] = alpha * acc_ref[:, lo:] + jnp.dot(vt, p, preferred_element_type=F32)
        m_ref[:, lo:] = m_new

    def diagonal_mask(s):
        kc = lax.broadcasted_iota(jnp.int32, s.shape, 0) // CHUNK
        qc = lax.broadcasted_iota(jnp.int32, s.shape, 1) // CHUNK
        return jnp.where(kc <= qc, s, NEG_INF)

    for u in range(B_AHEAD):
        scores(u, u, 0)

    def query_tile(qi, carry):
        q0 = qi * B_TQ
        m_ref[...] = jnp.full(m_ref.shape, NEG_INF, F32)
        acc_ref[...] = jnp.zeros(acc_ref.shape, F32)

        def body(t, c):
            for u in range(B_NB):
                scores(t * B_NB + u + B_AHEAD, (u + B_AHEAD) % B_NB, q0)
                accumulate(t * B_NB + u, s_refs[u][...], mx_refs[u][...])
            return c

        lax.fori_loop(0, qi, body, 0)
        q0_next = jnp.minimum(qi + 1, n_q - 1) * B_TQ
        for u in range(B_NB):
            lo = u * B_TK
            if u + B_AHEAD < B_NB:
                scores(qi * B_NB + u + B_AHEAD, u + B_AHEAD, q0, lo=(u + B_AHEAD) * B_TK)
            else:
                scores(u + B_AHEAD - B_NB, u + B_AHEAD - B_NB, q0_next)
            s = diagonal_mask(s_refs[u][:, lo:])
            accumulate(qi * B_NB + u, s, jnp.max(s, axis=0, keepdims=True), lo=lo)
        denom = acc_ref[B_V_DIM:B_V_DIM + 1, :]
        o_ref[0, :, pl.ds(pl.multiple_of(q0, B_TQ), B_TQ)] = (acc_ref[0:B_V_DIM, :] * (1.0 / denom)).astype(BF16)
        return carry

    lax.fori_loop(0, n_q, query_tile, 0)


def _mixer_b(qb, kb, vbt):
    B, S, _ = qb.shape
    return pl.pallas_call(
        _mixer_b_kernel,
        grid=(B, B_HEADS),
        in_specs=[pl.BlockSpec((1, S, LANES), lambda b, h: (b, 0, h)),
                  pl.BlockSpec((1, S, LANES), lambda b, h: (b, 0, h)),
                  pl.BlockSpec((1, VT_ROWS, S), lambda b, h: (b, h, 0))],
        out_specs=pl.BlockSpec((1, B_V_DIM, S), lambda b, h: (b, h, 0)),
        out_shape=jax.ShapeDtypeStruct((B, B_WIDTH, S), BF16),
        scratch_shapes=[pltpu.VMEM((B_TK, B_TQ), F32)] * B_NB + [pltpu.VMEM((1, B_TQ), F32)] * B_NB
                       + [pltpu.VMEM((1, B_TQ), F32), pltpu.VMEM((VT_ROWS, B_TQ), F32)],
        compiler_params=pltpu.CompilerParams(dimension_semantics=("arbitrary", "arbitrary"),
                                             vmem_limit_bytes=VMEM_LIMIT),
        name="mixer_b",
    )(qb, kb, vbt)


def _merge_kernel(x_ref, yat_ref, ybt_ref, lng_ref, lnb_ref, wzg_ref, bzg_ref, wpa_ref, wpb_ref, wout_ref,
                  pg_ref, pb_ref, o_ref):
    rows = x_ref.shape[1] // MERGE_GROUPS
    for g in range(MERGE_GROUPS):
        r = slice(g * rows, (g + 1) * rows)
        h = _layer_norm(x_ref[0, r, :], lng_ref[...], lnb_ref[...])
        zg = jnp.dot(h.astype(BF16), wzg_ref[...], preferred_element_type=F32) + bzg_ref[...]
        za = zg[:, 0:A_WIDTH]
        zb = zg[:, A_WIDTH:A_WIDTH + B_WIDTH]
        ga = zg[:, A_WIDTH + B_WIDTH:A_WIDTH + B_WIDTH + D_MODEL]
        gb = zg[:, A_WIDTH + B_WIDTH + D_MODEL:]
        ya_in = yat_ref[0, :, r].astype(F32).T * (za * jax.nn.sigmoid(za))
        yb_in = ybt_ref[0, :, r].astype(F32).T * (zb * jax.nn.sigmoid(zb))
        ya = jnp.dot(ya_in.astype(BF16), wpa_ref[...], preferred_element_type=F32)
        yb = jnp.dot(yb_in.astype(BF16), wpb_ref[...], preferred_element_type=F32)
        mixed = jax.nn.sigmoid(ga) * ya + jax.nn.sigmoid(gb) * yb
        out = jnp.dot(mixed.astype(BF16), wout_ref[...], preferred_element_type=F32)
        o_ref[0, r, :] = _layer_norm(DEEPNORM_ALPHA * h + out, pg_ref[...], pb_ref[...])


def _merge(x, yat, ybt, lng, lnb, wzg, bzg, wpa, wpb, wout, pg, pb):
    B, S, _ = x.shape
    tm = OUT_TM
    row = lambda w: pl.BlockSpec((1, tm, w), lambda b, i: (b, i, 0))
    nzg = wzg.shape[1]
    return pl.pallas_call(
        _merge_kernel,
        grid=(B, S // tm),
        in_specs=[row(D_MODEL), pl.BlockSpec((1, A_WIDTH, tm), lambda b, i: (b, 0, i)),
                  pl.BlockSpec((1, B_WIDTH, tm), lambda b, i: (b, 0, i)),
                  _const_spec((1, D_MODEL)), _const_spec((1, D_MODEL)),
                  _const_spec((D_MODEL, nzg)), _const_spec((1, nzg)),
                  _const_spec((A_WIDTH, D_MODEL)), _const_spec((B_WIDTH, D_MODEL)),
                  _const_spec((D_MODEL, D_MODEL)), _const_spec((1, D_MODEL)), _const_spec((1, D_MODEL))],
        out_specs=row(D_MODEL),
        out_shape=jax.ShapeDtypeStruct((B, S, D_MODEL), F32),
        compiler_params=pltpu.CompilerParams(dimension_semantics=("arbitrary", "arbitrary"),
                                             vmem_limit_bytes=VMEM_LIMIT),
        name="merge_out",
    )(x, yat, ybt, lng, lnb, wzg, bzg, wpa, wpb, wout, pg, pb)


def _rot_cols(w):
    half = w.shape[-1] // 2
    return jnp.concatenate([-w[..., half:], w[..., :half]], axis=-1)


def _prep_layer(w_in, b_in, w_uq, w_ukv):
    c = 0
    cols = {}
    for name, width in (("aq", A_WIDTH), ("ak", A_WIDTH), ("av", A_WIDTH), ("az", A_WIDTH), ("cq", Q_LORA),
                        ("ckv", KV_LORA), ("kr", B_ROPE_DIM), ("bz", B_WIDTH), ("ga", D_MODEL), ("gb", D_MODEL)):
        cols[name] = slice(c, c + width)
        c += width
    w = lambda n: w_in[:, cols[n]]
    b = lambda n: b_in[cols[n]]
    a_scale = A_HEAD_DIM ** -0.5 * LOG2E
    zeros_w = jnp.zeros((D_MODEL, B_NOPE_DIM), F32)
    zeros_b = jnp.zeros((B_NOPE_DIM,), F32)
    w1 = jnp.concatenate([w("aq") * a_scale, w("ak"), w("cq"), w("ckv"),
                          zeros_w, w("kr"), w("kr"),
                          zeros_w, _rot_cols(w("kr")), _rot_cols(w("kr"))], axis=1)
    b1 = jnp.concatenate([b("aq") * a_scale, b("ak"), b("cq"), b("ckv"),
                          zeros_b, b("kr"), b("kr"),
                          zeros_b, _rot_cols(b("kr")), _rot_cols(b("kr"))])
    wzg = jnp.concatenate([w("az"), w("bz"), w("ga"), w("gb")], axis=1)
    bzg = jnp.concatenate([b("az"), b("bz"), b("ga"), b("gb")])

    uq = w_uq.reshape(Q_LORA, B_HEADS, B_QK_DIM)
    uq_rope = uq[:, :, B_NOPE_DIM:]
    wq = jnp.concatenate([uq, _rot_cols(uq_rope)], axis=-1).reshape(Q_LORA, B_HEADS * LANES)
    ukv = w_ukv.reshape(KV_LORA, B_HEADS, B_NOPE_DIM + B_V_DIM)
    wk = jnp.concatenate([ukv[:, :, :B_NOPE_DIM], jnp.zeros((KV_LORA, B_HEADS, LANES - B_NOPE_DIM), F32)],
                         axis=-1).reshape(KV_LORA, B_HEADS * LANES)
    wvt = ukv[:, :, B_NOPE_DIM:].reshape(KV_LORA, B_WIDTH).T
    return (w1.astype(BF16), b1[None, :], w("av").T.astype(BF16), b("av")[:, None], wq.astype(BF16),
            wk.astype(BF16), wvt.astype(BF16), wzg.astype(BF16), bzg[None, :])


def _rope_freq_row():
    half = B_ROPE_DIM // 2
    inv_freq = ROPE_THETA ** (-jnp.arange(half, dtype=F32) / half)
    return jnp.concatenate([jnp.zeros((B_NOPE_DIM,), F32), inv_freq, inv_freq, inv_freq, inv_freq])[None, :]


def kernel(x, positions, ln_in_g, ln_in_b, w_in, b_in, q_norm_g, kv_norm_g, w_uq, w_ukv, rel_bias, w_proj_a,
           w_proj_b, w_out, ln_post_g, ln_post_b):
    depth = w_in.shape[0]
    assert depth == 1, "the trunk-entry norm is recomputed per kernel, which is only valid for one layer"
    B, S, _ = x.shape
    pos3 = positions.reshape(B, 1, S)
    lng, lnb = ln_in_g[None, :], ln_in_b[None, :]
    freq = _rope_freq_row()
    l = 0
    w1, b1, wavt, bav, wq, wk, wvt, wzg, bzg = _prep_layer(w_in[l], b_in[l], w_uq[l], w_ukv[l])
    aq, ak, avt, qb, kb, vbt = _token_projections(x, pos3, lng, lnb, w1, b1, wavt, bav, q_norm_g[l][None, :],
                                                  kv_norm_g[l][None, :], wq, wk, wvt, freq)
    yat = _mixer_a(aq, ak, avt, _mixer_a_bias(rel_bias[l]))
    ybt = _mixer_b(qb, kb, vbt)
    return _merge(x, yat, ybt, lng, lnb, wzg, bzg, w_proj_a[l].astype(BF16), w_proj_b[l].astype(BF16),
                  w_out[l].astype(BF16), ln_post_g[l][None, :], ln_post_b[l][None, :])
```

```python
import jax
import jax.numpy as jnp
from jax import lax
from jax.experimental import pallas as pl
from jax.experimental.pallas import tpu as pltpu

D_MODEL = 1024
CHUNK = 64
A_HEADS = 8
A_HEAD_DIM = 64
A_WIDTH = A_HEADS * A_HEAD_DIM
A_LEFT_CHUNKS = 8
REL_CLIP = 128
B_HEADS = 8
B_NOPE_DIM = 64
B_ROPE_DIM = 32
B_QK_DIM = B_NOPE_DIM + B_ROPE_DIM
B_V_DIM = 64
B_WIDTH = B_HEADS * B_V_DIM
Q_LORA = 256
KV_LORA = 128
ROPE_THETA = 10000.0
DEEPNORM_ALPHA = 2.0 ** 0.25
LN_EPS = 1e-5
RMS_EPS = 1e-6
NEG_INF = -1e30

LANES = 128
VMEM_LIMIT = 56 * 1024 * 1024

PROJ_TM = 1024
A_TQ = 256
A_WIN = A_TQ + A_LEFT_CHUNKS * CHUNK
A_ROLL = 1024
B_TQ = 1024
B_TK = 256
B_NB = B_TQ // B_TK
B_AHEAD = 2
V_DIM = 64
VT_ROWS = 80
LOG2E = 1.4426950408889634
OUT_TM = 512
MERGE_GROUPS = 2

BF16 = jnp.bfloat16
F32 = jnp.float32


def _layer_norm(x, g, b):
    mu = jnp.mean(x, axis=-1, keepdims=True)
    xc = x - mu
    var = jnp.mean(xc * xc, axis=-1, keepdims=True)
    return xc * lax.rsqrt(var + LN_EPS) * g + b


def _rms_norm(x, g):
    return x * lax.rsqrt(jnp.mean(x * x, axis=-1, keepdims=True) + RMS_EPS) * g


def _const_spec(shape):
    nd = len(shape)
    return pl.BlockSpec(shape, lambda *_: (0,) * nd, pipeline_mode=pl.Buffered(1))


def _store_values_t(ref, vt):
    tail = VT_ROWS - V_DIM
    ones_rows = (lax.broadcasted_iota(jnp.int32, (tail, vt.shape[1]), 0) == 0).astype(BF16)
    for hd in range(vt.shape[0] // V_DIM):
        ref[0, hd * VT_ROWS:hd * VT_ROWS + V_DIM, :] = vt[hd * V_DIM:(hd + 1) * V_DIM].astype(BF16)
        ref[0, hd * VT_ROWS + V_DIM:(hd + 1) * VT_ROWS, :] = ones_rows


def _proj_kernel(x_ref, pos_ref, lng_ref, lnb_ref, w1_ref, b1_ref, wavt_ref, bav_ref, qg_ref, kvg_ref, wq_ref,
                 wk_ref, wvt_ref, freq_ref, aq_ref, ak_ref, avt_ref, qb_ref, kb_ref, vbt_ref):
    hb = _layer_norm(x_ref[0], lng_ref[...], lnb_ref[...]).astype(BF16)
    proj = jnp.dot(hb, w1_ref[...], preferred_element_type=F32) + b1_ref[...]
    aq_ref[0] = proj[:, 0:A_WIDTH].astype(BF16)
    ak_ref[0] = proj[:, A_WIDTH:2 * A_WIDTH].astype(BF16)
    avt = lax.dot_general(wavt_ref[...], hb, (((1,), (1,)), ((), ())), preferred_element_type=F32) + bav_ref[...]
    _store_values_t(avt_ref, avt)
    o = 2 * A_WIDTH
    cq = _rms_norm(proj[:, o:o + Q_LORA], qg_ref[...]).astype(BF16)
    o += Q_LORA
    ckv = _rms_norm(proj[:, o:o + KV_LORA], kvg_ref[...]).astype(BF16)
    o += KV_LORA
    kr = proj[:, o:o + LANES]
    kr_rot = proj[:, o + LANES:o + 2 * LANES]

    pos_rows = jnp.broadcast_to(pos_ref[0].astype(F32), (LANES, pos_ref.shape[2])).T
    ang = pos_rows * freq_ref[...]
    cos_t = jnp.cos(ang)
    sin_t = jnp.sin(ang)
    lane = lax.broadcasted_iota(jnp.int32, (1, LANES), 1)
    scale = B_QK_DIM ** -0.5 * LOG2E
    tq = jnp.where(lane < B_NOPE_DIM, scale, jnp.where(lane < B_QK_DIM, cos_t, sin_t) * scale)
    k_rope = kr * cos_t + kr_rot * sin_t

    q_all = jnp.dot(cq, wq_ref[...], preferred_element_type=F32)
    k_all = jnp.dot(ckv, wk_ref[...], preferred_element_type=F32)
    for hd in range(B_HEADS):
        sl = slice(hd * LANES, (hd + 1) * LANES)
        qb_ref[0, :, sl] = (q_all[:, sl] * tq).astype(BF16)
        kb_ref[0, :, sl] = (k_all[:, sl] + k_rope).astype(BF16)
    vt = lax.dot_general(wvt_ref[...], ckv, (((1,), (1,)), ((), ())), preferred_element_type=F32)
    _store_values_t(vbt_ref, vt)


def _token_projections(x, pos3, lng, lnb, w1, b1, wavt, bav, qg, kvg, wq, wk, wvt, freq):
    B, S, _ = x.shape
    tm = PROJ_TM
    n1 = w1.shape[1]
    row = lambda w: pl.BlockSpec((1, tm, w), lambda b, i: (b, i, 0))
    vt_spec = pl.BlockSpec((1, B_HEADS * VT_ROWS, tm), lambda b, i: (b, 0, i))
    vt_shape = jax.ShapeDtypeStruct((B, B_HEADS * VT_ROWS, S), BF16)
    return pl.pallas_call(
        _proj_kernel,
        grid=(B, S // tm),
        in_specs=[row(D_MODEL), pl.BlockSpec((1, 1, tm), lambda b, i: (b, 0, i)),
                  _const_spec((1, D_MODEL)), _const_spec((1, D_MODEL)),
                  _const_spec((D_MODEL, n1)), _const_spec((1, n1)),
                  _const_spec((A_WIDTH, D_MODEL)), _const_spec((A_WIDTH, 1)),
                  _const_spec((1, Q_LORA)), _const_spec((1, KV_LORA)),
                  _const_spec((Q_LORA, B_HEADS * LANES)), _const_spec((KV_LORA, B_HEADS * LANES)),
                  _const_spec((B_WIDTH, KV_LORA)), _const_spec((1, LANES))],
        out_specs=[row(A_WIDTH), row(A_WIDTH), vt_spec, row(B_HEADS * LANES), row(B_HEADS * LANES), vt_spec],
        out_shape=[jax.ShapeDtypeStruct((B, S, A_WIDTH), BF16)] * 2 + [vt_shape]
                  + [jax.ShapeDtypeStruct((B, S, B_HEADS * LANES), BF16)] * 2 + [vt_shape],
        compiler_params=pltpu.CompilerParams(dimension_semantics=("arbitrary", "arbitrary"),
                                             vmem_limit_bytes=VMEM_LIMIT),
        name="token_projections",
    )(x, pos3, lng, lnb, w1, b1, wavt, bav, qg, kvg, wq, wk, wvt, freq)


def _mixer_a_kernel(q_ref, k2_ref, k1_ref, k0_ref, v2_ref, v1_ref, v0_ref, bias_ref, o_ref,
                    s0_ref, s1_ref, mx0_ref, mx1_ref):
    i = pl.program_id(1)
    s_refs, mx_refs = (s0_ref, s1_ref), (mx0_ref, mx1_ref)
    n_old = A_LEFT_CHUNKS * CHUNK
    kpos = lax.broadcasted_iota(jnp.int32, (n_old, 1), 0) + (i * A_TQ - n_old)
    pad_mask = jnp.where(kpos >= 0, 0.0, NEG_INF).astype(F32)
    lane = lax.broadcasted_iota(jnp.int32, (1, LANES), 1)

    def scores(pair, slot):
        sl = slice(pair * LANES, (pair + 1) * LANES)
        qp = q_ref[0, :, sl]
        kp = jnp.concatenate([k2_ref[0, :, sl], k1_ref[0, :, sl], k0_ref[0, :, sl]], axis=0)
        q2 = jnp.concatenate([jnp.where((lane // A_HEAD_DIM) == hh, qp, jnp.zeros_like(qp)) for hh in range(2)],
                             axis=0)
        s = lax.dot_general(kp, q2, (((1,), (1,)), ((), ())), preferred_element_type=F32)
        s = s + jnp.concatenate([bias_ref[2 * pair], bias_ref[2 * pair + 1]], axis=1)
        s = jnp.concatenate([s[:n_old] + pad_mask, s[n_old:]], axis=0)
        s_refs[slot][...] = s
        mx_refs[slot][...] = jnp.max(s, axis=0, keepdims=True)

    def finish(pair, slot):
        p = jnp.exp2(s_refs[slot][...] - mx_refs[slot][...]).astype(BF16)
        rows = slice(2 * pair * VT_ROWS, (2 * pair + 2) * VT_ROWS)
        vt = jnp.concatenate([v2_ref[0, rows, :], v1_ref[0, rows, :], v0_ref[0, rows, :]], axis=1)
        for hh in range(2):
            oh = jnp.dot(vt[hh * VT_ROWS:(hh + 1) * VT_ROWS], p[:, hh * A_TQ:(hh + 1) * A_TQ],
                         preferred_element_type=F32)
            hd = 2 * pair + hh
            o_ref[0, hd * V_DIM:(hd + 1) * V_DIM, :] = (oh[:V_DIM] * (1.0 / oh[V_DIM:V_DIM + 1])).astype(BF16)

    n_pairs = A_HEADS // 2
    scores(0, 0)
    for pair in range(n_pairs):
        if pair + 1 < n_pairs:
            scores(pair + 1, (pair + 1) % 2)
        finish(pair, pair % 2)


def _mixer_a(aq, ak, avt, bias):
    B, S, W = aq.shape
    tq = A_TQ
    blk = lambda back: pl.BlockSpec((1, tq, W), lambda b, i: (b, jnp.maximum(i - back, 0), 0))
    vblk = lambda back: pl.BlockSpec((1, A_HEADS * VT_ROWS, tq), lambda b, i: (b, 0, jnp.maximum(i - back, 0)))
    return pl.pallas_call(
        _mixer_a_kernel,
        grid=(B, S // tq),
        in_specs=[blk(0), blk(2), blk(1), blk(0), vblk(2), vblk(1), vblk(0), _const_spec(bias.shape)],
        out_specs=pl.BlockSpec((1, W, tq), lambda b, i: (b, 0, i)),
        out_shape=jax.ShapeDtypeStruct((B, W, S), BF16),
        scratch_shapes=[pltpu.VMEM((A_WIN, 2 * tq), F32)] * 2 + [pltpu.VMEM((1, 2 * tq), F32)] * 2,
        compiler_params=pltpu.CompilerParams(dimension_semantics=("arbitrary", "arbitrary"),
                                             vmem_limit_bytes=VMEM_LIMIT),
        name="mixer_a",
    )(aq, ak, ak, ak, avt, avt, avt, bias)


def _bias_kernel(row_ref, o_ref):
    rows = jnp.broadcast_to(row_ref[0], (A_WIN, A_ROLL))
    table = pltpu.roll(rows, 0, 1, stride=1, stride_axis=0)[:, :A_TQ]
    kc = lax.broadcasted_iota(jnp.int32, (A_WIN, A_TQ), 0) // CHUNK
    qc = lax.broadcasted_iota(jnp.int32, (A_WIN, A_TQ), 1) // CHUNK
    gap = qc + A_LEFT_CHUNKS - kc
    o_ref[0] = jnp.where((gap >= 0) & (gap <= A_LEFT_CHUNKS), table * LOG2E, NEG_INF)


def _mixer_a_bias(rel_bias):
    tbl = rel_bias.T.astype(F32)
    n_old = A_LEFT_CHUNKS * CHUNK
    first_tbl = A_ROLL - n_old - REL_CLIP
    far = jnp.broadcast_to(tbl[:, -1:], (A_HEADS, A_ROLL))
    near = jnp.broadcast_to(tbl[:, :1], (A_HEADS, A_ROLL))
    row = jnp.concatenate([far[:, :A_ROLL - A_WIN + 1], near[:, A_ROLL - A_WIN + 1:first_tbl], tbl,
                           far[:, first_tbl + tbl.shape[1]:]], axis=1)
    row = row[:, None, :]
    return pl.pallas_call(
        _bias_kernel,
        grid=(A_HEADS,),
        in_specs=[pl.BlockSpec((1, 1, A_ROLL), lambda h: (h, 0, 0))],
        out_specs=pl.BlockSpec((1, A_WIN, A_TQ), lambda h: (h, 0, 0)),
        out_shape=jax.ShapeDtypeStruct((A_HEADS, A_WIN, A_TQ), F32),
        name="mixer_a_bias",
    )(row)


def _mixer_b_kernel(q_ref, k_ref, vt_ref, o_ref, *scratch):
    s_refs = scratch[0:B_NB]
    mx_refs = scratch[B_NB:2 * B_NB]
    m_ref, acc_ref = scratch[2 * B_NB:]
    n_q = q_ref.shape[1] // B_TQ

    def scores(blk, slot, q0, lo=0):
        start = pl.multiple_of(blk * B_TK, B_TK)
        k = k_ref[0, pl.ds(start, B_TK), :]
        q = q_ref[0, pl.ds(pl.multiple_of(q0 + lo, B_TK), B_TQ - lo), :]
        s = lax.dot_general(k, q, (((1,), (1,)), ((), ())), preferred_element_type=F32)
        s_refs[slot][:, lo:] = s
        mx_refs[slot][:, lo:] = jnp.max(s, axis=0, keepdims=True)

    def accumulate(blk, s, mx, lo=0):
        start = pl.multiple_of(blk * B_TK, B_TK)
        m_old = m_ref[:, lo:]
        m_new = jnp.maximum(m_old, mx)
        alpha = jnp.exp2(m_old - m_new)
        p = jnp.exp2(s - m_new).astype(BF16)
        vt = vt_ref[0, :, pl.ds(start, B_TK)]
        acc_ref[:, lo:] = alpha * acc_ref[:, lo:] + jnp.dot(vt, p, preferred_element_type=F32)
        m_ref[:, lo:] = m_new

    def diagonal_mask(s):
        kc = lax.broadcasted_iota(jnp.int32, s.shape, 0) // CHUNK
        qc = lax.broadcasted_iota(jnp.int32, s.shape, 1) // CHUNK
        return jnp.where(kc <= qc, s, NEG_INF)

    for u in range(B_AHEAD):
        scores(u, u, 0)

    def query_tile(qi, carry):
        q0 = qi * B_TQ
        m_ref[...] = jnp.full(m_ref.shape, NEG_INF, F32)
        acc_ref[...] = jnp.zeros(acc_ref.shape, F32)

        def steps(first_blk, n_blk):
            for u in range(n_blk):
                scores(first_blk + u + B_AHEAD, (u + B_AHEAD) % B_NB, q0)
                accumulate(first_blk + u, s_refs[u % B_NB][...], mx_refs[u % B_NB][...])

        def body(t, c):
            steps(t * 2 * B_NB, 2 * B_NB)
            return c

        lax.fori_loop(0, qi // 2, body, 0)

        @pl.when(qi % 2 == 1)
        def _():
            steps((qi - 1) * B_NB, B_NB)

        q0_next = jnp.minimum(qi + 1, n_q - 1) * B_TQ
        for u in range(B_NB):
            lo = u * B_TK
            if u + B_AHEAD < B_NB:
                scores(qi * B_NB + u + B_AHEAD, u + B_AHEAD, q0, lo=(u + B_AHEAD) * B_TK)
            else:
                scores(u + B_AHEAD - B_NB, u + B_AHEAD - B_NB, q0_next)
            s = diagonal_mask(s_refs[u][:, lo:])
            accumulate(qi * B_NB + u, s, jnp.max(s, axis=0, keepdims=True), lo=lo)
        denom = acc_ref[B_V_DIM:B_V_DIM + 1, :]
        o_ref[0, :, pl.ds(pl.multiple_of(q0, B_TQ), B_TQ)] = (acc_ref[0:B_V_DIM, :] * (1.0 / denom)).astype(BF16)
        return carry

    lax.fori_loop(0, n_q, query_tile, 0)


def _mixer_b(qb, kb, vbt):
    B, S, _ = qb.shape
    return pl.pallas_call(
        _mixer_b_kernel,
        grid=(B, B_HEADS),
        in_specs=[pl.BlockSpec((1, S, LANES), lambda b, h: (b, 0, h)),
                  pl.BlockSpec((1, S, LANES), lambda b, h: (b, 0, h)),
                  pl.BlockSpec((1, VT_ROWS, S), lambda b, h: (b, h, 0))],
        out_specs=pl.BlockSpec((1, B_V_DIM, S), lambda b, h: (b, h, 0)),
        out_shape=jax.ShapeDtypeStruct((B, B_WIDTH, S), BF16),
        scratch_shapes=[pltpu.VMEM((B_TK, B_TQ), F32)] * B_NB + [pltpu.VMEM((1, B_TQ), F32)] * B_NB
                       + [pltpu.VMEM((1, B_TQ), F32), pltpu.VMEM((VT_ROWS, B_TQ), F32)],
        compiler_params=pltpu.CompilerParams(dimension_semantics=("arbitrary", "arbitrary"),
                                             vmem_limit_bytes=VMEM_LIMIT),
        name="mixer_b",
    )(qb, kb, vbt)


def _merge_kernel(x_ref, yat_ref, ybt_ref, lng_ref, lnb_ref, wzg_ref, bzg_ref, wpa_ref, wpb_ref, wout_ref,
                  pg_ref, pb_ref, o_ref):
    rows = x_ref.shape[1] // MERGE_GROUPS
    for g in range(MERGE_GROUPS):
        r = slice(g * rows, (g + 1) * rows)
        h = _layer_norm(x_ref[0, r, :], lng_ref[...], lnb_ref[...])
        zg = jnp.dot(h.astype(BF16), wzg_ref[...], preferred_element_type=F32) + bzg_ref[...]
        za = zg[:, 0:A_WIDTH]
        zb = zg[:, A_WIDTH:A_WIDTH + B_WIDTH]
        ga = zg[:, A_WIDTH + B_WIDTH:A_WIDTH + B_WIDTH + D_MODEL]
        gb = zg[:, A_WIDTH + B_WIDTH + D_MODEL:]
        ya_in = yat_ref[0, :, r].astype(F32).T * (za * jax.nn.sigmoid(za))
        yb_in = ybt_ref[0, :, r].astype(F32).T * (zb * jax.nn.sigmoid(zb))
        ya = jnp.dot(ya_in.astype(BF16), wpa_ref[...], preferred_element_type=F32)
        yb = jnp.dot(yb_in.astype(BF16), wpb_ref[...], preferred_element_type=F32)
        mixed = jax.nn.sigmoid(ga) * ya + jax.nn.sigmoid(gb) * yb
        out = jnp.dot(mixed.astype(BF16), wout_ref[...], preferred_element_type=F32)
        o_ref[0, r, :] = _layer_norm(DEEPNORM_ALPHA * h + out, pg_ref[...], pb_ref[...])


def _merge(x, yat, ybt, lng, lnb, wzg, bzg, wpa, wpb, wout, pg, pb):
    B, S, _ = x.shape
    tm = OUT_TM
    row = lambda w: pl.BlockSpec((1, tm, w), lambda b, i: (b, i, 0))
    nzg = wzg.shape[1]
    return pl.pallas_call(
        _merge_kernel,
        grid=(B, S // tm),
        in_specs=[row(D_MODEL), pl.BlockSpec((1, A_WIDTH, tm), lambda b, i: (b, 0, i)),
                  pl.BlockSpec((1, B_WIDTH, tm), lambda b, i: (b, 0, i)),
                  _const_spec((1, D_MODEL)), _const_spec((1, D_MODEL)),
                  _const_spec((D_MODEL, nzg)), _const_spec((1, nzg)),
                  _const_spec((A_WIDTH, D_MODEL)), _const_spec((B_WIDTH, D_MODEL)),
                  _const_spec((D_MODEL, D_MODEL)), _const_spec((1, D_MODEL)), _const_spec((1, D_MODEL))],
        out_specs=row(D_MODEL),
        out_shape=jax.ShapeDtypeStruct((B, S, D_MODEL), F32),
        compiler_params=pltpu.CompilerParams(dimension_semantics=("arbitrary", "arbitrary"),
                                             vmem_limit_bytes=VMEM_LIMIT),
        name="merge_out",
    )(x, yat, ybt, lng, lnb, wzg, bzg, wpa, wpb, wout, pg, pb)


def _rot_cols(w):
    half = w.shape[-1] // 2
    return jnp.concatenate([-w[..., half:], w[..., :half]], axis=-1)


def _prep_layer(w_in, b_in, w_uq, w_ukv):
    c = 0
    cols = {}
    for name, width in (("aq", A_WIDTH), ("ak", A_WIDTH), ("av", A_WIDTH), ("az", A_WIDTH), ("cq", Q_LORA),
                        ("ckv", KV_LORA), ("kr", B_ROPE_DIM), ("bz", B_WIDTH), ("ga", D_MODEL), ("gb", D_MODEL)):
        cols[name] = slice(c, c + width)
        c += width
    w = lambda n: w_in[:, cols[n]]
    b = lambda n: b_in[cols[n]]
    a_scale = A_HEAD_DIM ** -0.5 * LOG2E
    zeros_w = jnp.zeros((D_MODEL, B_NOPE_DIM), F32)
    zeros_b = jnp.zeros((B_NOPE_DIM,), F32)
    w1 = jnp.concatenate([w("aq") * a_scale, w("ak"), w("cq"), w("ckv"),
                          zeros_w, w("kr"), w("kr"),
                          zeros_w, _rot_cols(w("kr")), _rot_cols(w("kr"))], axis=1)
    b1 = jnp.concatenate([b("aq") * a_scale, b("ak"), b("cq"), b("ckv"),
                          zeros_b, b("kr"), b("kr"),
                          zeros_b, _rot_cols(b("kr")), _rot_cols(b("kr"))])
    wzg = jnp.concatenate([w("az"), w("bz"), w("ga"), w("gb")], axis=1)
    bzg = jnp.concatenate([b("az"), b("bz"), b("ga"), b("gb")])

    uq = w_uq.reshape(Q_LORA, B_HEADS, B_QK_DIM)
    uq_rope = uq[:, :, B_NOPE_DIM:]
    wq = jnp.concatenate([uq, _rot_cols(uq_rope)], axis=-1).reshape(Q_LORA, B_HEADS * LANES)
    ukv = w_ukv.reshape(KV_LORA, B_HEADS, B_NOPE_DIM + B_V_DIM)
    wk = jnp.concatenate([ukv[:, :, :B_NOPE_DIM], jnp.zeros((KV_LORA, B_HEADS, LANES - B_NOPE_DIM), F32)],
                         axis=-1).reshape(KV_LORA, B_HEADS * LANES)
    wvt = ukv[:, :, B_NOPE_DIM:].reshape(KV_LORA, B_WIDTH).T
    return (w1.astype(BF16), b1[None, :], w("av").T.astype(BF16), b("av")[:, None], wq.astype(BF16),
            wk.astype(BF16), wvt.astype(BF16), wzg.astype(BF16), bzg[None, :])


def _rope_freq_row():
    half = B_ROPE_DIM // 2
    inv_freq = ROPE_THETA ** (-jnp.arange(half, dtype=F32) / half)
    return jnp.concatenate([jnp.zeros((B_NOPE_DIM,), F32), inv_freq, inv_freq, inv_freq, inv_freq])[None, :]


def kernel(x, positions, ln_in_g, ln_in_b, w_in, b_in, q_norm_g, kv_norm_g, w_uq, w_ukv, rel_bias, w_proj_a,
           w_proj_b, w_out, ln_post_g, ln_post_b):
    depth = w_in.shape[0]
    assert depth == 1, "the trunk-entry norm is recomputed per kernel, which is only valid for one layer"
    B, S, _ = x.shape
    pos3 = positions.reshape(B, 1, S)
    lng, lnb = ln_in_g[None, :], ln_in_b[None, :]
    freq = _rope_freq_row()
    l = 0
    w1, b1, wavt, bav, wq, wk, wvt, wzg, bzg = _prep_layer(w_in[l], b_in[l], w_uq[l], w_ukv[l])
    aq, ak, avt, qb, kb, vbt = _token_projections(x, pos3, lng, lnb, w1, b1, wavt, bav, q_norm_g[l][None, :],
                                                  kv_norm_g[l][None, :], wq, wk, wvt, freq)
    yat = _mixer_a(aq, ak, avt, _mixer_a_bias(rel_bias[l]))
    ybt = _mixer_b(qb, kb, vbt)
    return _merge(x, yat, ybt, lng, lnb, wzg, bzg, w_proj_a[l].astype(BF16), w_proj_b[l].astype(BF16),
                  w_out[l].astype(BF16), ln_post_g[l][None, :], ln_post_b[l][None, :])
```

```python
import jax
import jax.numpy as jnp
from jax import lax
from jax.experimental import pallas as pl
from jax.experimental.pallas import tpu as pltpu

D_MODEL = 1024
CHUNK = 64
A_HEADS = 8
A_HEAD_DIM = 64
A_WIDTH = A_HEADS * A_HEAD_DIM
A_LEFT_CHUNKS = 8
REL_CLIP = 128
B_HEADS = 8
B_NOPE_DIM = 64
B_ROPE_DIM = 32
B_QK_DIM = B_NOPE_DIM + B_ROPE_DIM
B_V_DIM = 64
B_WIDTH = B_HEADS * B_V_DIM
Q_LORA = 256
KV_LORA = 128
ROPE_THETA = 10000.0
DEEPNORM_ALPHA = 2.0 ** 0.25
LN_EPS = 1e-5
RMS_EPS = 1e-6
NEG_INF = -1e30

LANES = 128
VMEM_LIMIT = 56 * 1024 * 1024

PROJ_TM = 1024
A_TQ = 256
A_WIN = A_TQ + A_LEFT_CHUNKS * CHUNK
A_SUB = 2
A_STEP = A_SUB * A_TQ
A_ROLL = 1024
B_TQ = 1024
B_TK = 256
B_NB = B_TQ // B_TK
B_AHEAD = 2
V_DIM = 64
VT_ROWS = 80
LOG2E = 1.4426950408889634
OUT_TM = 512
MERGE_GROUPS = 2

BF16 = jnp.bfloat16
F32 = jnp.float32


def _layer_norm(x, g, b):
    mu = jnp.mean(x, axis=-1, keepdims=True)
    xc = x - mu
    var = jnp.mean(xc * xc, axis=-1, keepdims=True)
    return xc * lax.rsqrt(var + LN_EPS) * g + b


def _rms_norm(x, g):
    return x * lax.rsqrt(jnp.mean(x * x, axis=-1, keepdims=True) + RMS_EPS) * g


def _const_spec(shape):
    nd = len(shape)
    return pl.BlockSpec(shape, lambda *_: (0,) * nd, pipeline_mode=pl.Buffered(1))


def _store_values_t(ref, vt):
    tail = VT_ROWS - V_DIM
    ones_rows = (lax.broadcasted_iota(jnp.int32, (tail, vt.shape[1]), 0) == 0).astype(BF16)
    for hd in range(vt.shape[0] // V_DIM):
        ref[0, hd * VT_ROWS:hd * VT_ROWS + V_DIM, :] = vt[hd * V_DIM:(hd + 1) * V_DIM].astype(BF16)
        ref[0, hd * VT_ROWS + V_DIM:(hd + 1) * VT_ROWS, :] = ones_rows


def _proj_kernel(x_ref, pos_ref, lng_ref, lnb_ref, w1_ref, b1_ref, wavt_ref, bav_ref, qg_ref, kvg_ref, wq_ref,
                 wk_ref, wvt_ref, freq_ref, aq_ref, ak_ref, avt_ref, qb_ref, kb_ref, vbt_ref):
    hb = _layer_norm(x_ref[0], lng_ref[...], lnb_ref[...]).astype(BF16)
    proj = jnp.dot(hb, w1_ref[...], preferred_element_type=F32) + b1_ref[...]
    aq_ref[0] = proj[:, 0:A_WIDTH].astype(BF16)
    ak_ref[0] = proj[:, A_WIDTH:2 * A_WIDTH].astype(BF16)
    avt = lax.dot_general(wavt_ref[...], hb, (((1,), (1,)), ((), ())), preferred_element_type=F32) + bav_ref[...]
    _store_values_t(avt_ref, avt)
    o = 2 * A_WIDTH
    cq = _rms_norm(proj[:, o:o + Q_LORA], qg_ref[...]).astype(BF16)
    o += Q_LORA
    ckv = _rms_norm(proj[:, o:o + KV_LORA], kvg_ref[...]).astype(BF16)
    o += KV_LORA
    kr = proj[:, o:o + LANES]
    kr_rot = proj[:, o + LANES:o + 2 * LANES]

    pos_rows = jnp.broadcast_to(pos_ref[0].astype(F32), (LANES, pos_ref.shape[2])).T
    ang = pos_rows * freq_ref[...]
    cos_t = jnp.cos(ang)
    sin_t = jnp.sin(ang)
    lane = lax.broadcasted_iota(jnp.int32, (1, LANES), 1)
    scale = B_QK_DIM ** -0.5 * LOG2E
    tq = jnp.where(lane < B_NOPE_DIM, scale, jnp.where(lane < B_QK_DIM, cos_t, sin_t) * scale)
    k_rope = kr * cos_t + kr_rot * sin_t

    q_all = jnp.dot(cq, wq_ref[...], preferred_element_type=F32)
    k_all = jnp.dot(ckv, wk_ref[...], preferred_element_type=F32)
    for hd in range(B_HEADS):
        sl = slice(hd * LANES, (hd + 1) * LANES)
        qb_ref[0, :, sl] = (q_all[:, sl] * tq).astype(BF16)
        kb_ref[0, :, sl] = (k_all[:, sl] + k_rope).astype(BF16)
    vt = lax.dot_general(wvt_ref[...], ckv, (((1,), (1,)), ((), ())), preferred_element_type=F32)
    _store_values_t(vbt_ref, vt)


def _token_projections(x, pos3, lng, lnb, w1, b1, wavt, bav, qg, kvg, wq, wk, wvt, freq):
    B, S, _ = x.shape
    tm = PROJ_TM
    n1 = w1.shape[1]
    row = lambda w: pl.BlockSpec((1, tm, w), lambda b, i: (b, i, 0))
    vt_spec = pl.BlockSpec((1, B_HEADS * VT_ROWS, tm), lambda b, i: (b, 0, i))
    vt_shape = jax.ShapeDtypeStruct((B, B_HEADS * VT_ROWS, S), BF16)
    return pl.pallas_call(
        _proj_kernel,
        grid=(B, S // tm),
        in_specs=[row(D_MODEL), pl.BlockSpec((1, 1, tm), lambda b, i: (b, 0, i)),
                  _const_spec((1, D_MODEL)), _const_spec((1, D_MODEL)),
                  _const_spec((D_MODEL, n1)), _const_spec((1, n1)),
                  _const_spec((A_WIDTH, D_MODEL)), _const_spec((A_WIDTH, 1)),
                  _const_spec((1, Q_LORA)), _const_spec((1, KV_LORA)),
                  _const_spec((Q_LORA, B_HEADS * LANES)), _const_spec((KV_LORA, B_HEADS * LANES)),
                  _const_spec((B_WIDTH, KV_LORA)), _const_spec((1, LANES))],
        out_specs=[row(A_WIDTH), row(A_WIDTH), vt_spec, row(B_HEADS * LANES), row(B_HEADS * LANES), vt_spec],
        out_shape=[jax.ShapeDtypeStruct((B, S, A_WIDTH), BF16)] * 2 + [vt_shape]
                  + [jax.ShapeDtypeStruct((B, S, B_HEADS * LANES), BF16)] * 2 + [vt_shape],
        compiler_params=pltpu.CompilerParams(dimension_semantics=("arbitrary", "arbitrary"),
                                             vmem_limit_bytes=VMEM_LIMIT),
        name="token_projections",
    )(x, pos3, lng, lnb, w1, b1, wavt, bav, qg, kvg, wq, wk, wvt, freq)


def _mixer_a_kernel(q_ref, k1_ref, k0_ref, v1_ref, v0_ref, bias_ref, o_ref, s0_ref, s1_ref, mx0_ref, mx1_ref):
    i = pl.program_id(1)
    s_refs, mx_refs = (s0_ref, s1_ref), (mx0_ref, mx1_ref)
    n_old = A_LEFT_CHUNKS * CHUNK
    lane = lax.broadcasted_iota(jnp.int32, (1, LANES), 1)
    row = lax.broadcasted_iota(jnp.int32, (n_old, 1), 0)

    def scores(unit, slot):
        t, pair = divmod(unit, A_HEADS // 2)
        sl = slice(pair * LANES, (pair + 1) * LANES)
        qp = q_ref[0, t * A_TQ:(t + 1) * A_TQ, sl]
        kwin = jnp.concatenate([k1_ref[0, :, sl], k0_ref[0, :, sl]], axis=0)
        kp = kwin[t * A_TQ:t * A_TQ + A_WIN]
        q2 = jnp.concatenate([jnp.where((lane // A_HEAD_DIM) == hh, qp, jnp.zeros_like(qp)) for hh in range(2)],
                             axis=0)
        s = lax.dot_general(kp, q2, (((1,), (1,)), ((), ())), preferred_element_type=F32)
        s = s + jnp.concatenate([bias_ref[2 * pair], bias_ref[2 * pair + 1]], axis=1)
        kpos = row + (i * A_STEP + t * A_TQ - n_old)
        pad_mask = jnp.where(kpos >= 0, 0.0, NEG_INF).astype(F32)
        s = jnp.concatenate([s[:n_old] + pad_mask, s[n_old:]], axis=0)
        s_refs[slot][...] = s
        mx_refs[slot][...] = jnp.max(s, axis=0, keepdims=True)

    def finish(unit, slot):
        t, pair = divmod(unit, A_HEADS // 2)
        p = jnp.exp2(s_refs[slot][...] - mx_refs[slot][...]).astype(BF16)
        rows = slice(2 * pair * VT_ROWS, (2 * pair + 2) * VT_ROWS)
        vwin = jnp.concatenate([v1_ref[0, rows, :], v0_ref[0, rows, :]], axis=1)
        vt = vwin[:, t * A_TQ:t * A_TQ + A_WIN]
        for hh in range(2):
            oh = jnp.dot(vt[hh * VT_ROWS:(hh + 1) * VT_ROWS], p[:, hh * A_TQ:(hh + 1) * A_TQ],
                         preferred_element_type=F32)
            hd = 2 * pair + hh
            o_ref[0, hd * V_DIM:(hd + 1) * V_DIM, t * A_TQ:(t + 1) * A_TQ] = (
                oh[:V_DIM] * (1.0 / oh[V_DIM:V_DIM + 1])).astype(BF16)

    n_units = A_SUB * A_HEADS // 2
    scores(0, 0)
    for unit in range(n_units):
        if unit + 1 < n_units:
            scores(unit + 1, (unit + 1) % 2)
        finish(unit, unit % 2)


def _mixer_a(aq, ak, avt, bias):
    B, S, W = aq.shape
    blk = lambda back: pl.BlockSpec((1, A_STEP, W), lambda b, i: (b, jnp.maximum(i - back, 0), 0))
    vblk = lambda back: pl.BlockSpec((1, A_HEADS * VT_ROWS, A_STEP), lambda b, i: (b, 0, jnp.maximum(i - back, 0)))
    return pl.pallas_call(
        _mixer_a_kernel,
        grid=(B, S // A_STEP),
        in_specs=[blk(0), blk(1), blk(0), vblk(1), vblk(0), _const_spec(bias.shape)],
        out_specs=pl.BlockSpec((1, W, A_STEP), lambda b, i: (b, 0, i)),
        out_shape=jax.ShapeDtypeStruct((B, W, S), BF16),
        scratch_shapes=[pltpu.VMEM((A_WIN, 2 * A_TQ), F32)] * 2 + [pltpu.VMEM((1, 2 * A_TQ), F32)] * 2,
        compiler_params=pltpu.CompilerParams(dimension_semantics=("arbitrary", "arbitrary"),
                                             vmem_limit_bytes=VMEM_LIMIT),
        name="mixer_a",
    )(aq, ak, ak, avt, avt, bias)


def _bias_kernel(row_ref, o_ref):
    rows = jnp.broadcast_to(row_ref[0], (A_WIN, A_ROLL))
    table = pltpu.roll(rows, 0, 1, stride=1, stride_axis=0)[:, :A_TQ]
    kc = lax.broadcasted_iota(jnp.int32, (A_WIN, A_TQ), 0) // CHUNK
    qc = lax.broadcasted_iota(jnp.int32, (A_WIN, A_TQ), 1) // CHUNK
    gap = qc + A_LEFT_CHUNKS - kc
    o_ref[0] = jnp.where((gap >= 0) & (gap <= A_LEFT_CHUNKS), table * LOG2E, NEG_INF)


def _mixer_a_bias(rel_bias):
    tbl = rel_bias.T.astype(F32)
    n_old = A_LEFT_CHUNKS * CHUNK
    first_tbl = A_ROLL - n_old - REL_CLIP
    far = jnp.broadcast_to(tbl[:, -1:], (A_HEADS, A_ROLL))
    near = jnp.broadcast_to(tbl[:, :1], (A_HEADS, A_ROLL))
    row = jnp.concatenate([far[:, :A_ROLL - A_WIN + 1], near[:, A_ROLL - A_WIN + 1:first_tbl], tbl,
                           far[:, first_tbl + tbl.shape[1]:]], axis=1)
    row = row[:, None, :]
    return pl.pallas_call(
        _bias_kernel,
        grid=(A_HEADS,),
        in_specs=[pl.BlockSpec((1, 1, A_ROLL), lambda h: (h, 0, 0))],
        out_specs=pl.BlockSpec((1, A_WIN, A_TQ), lambda h: (h, 0, 0)),
        out_shape=jax.ShapeDtypeStruct((A_HEADS, A_WIN, A_TQ), F32),
        name="mixer_a_bias",
    )(row)


def _mixer_b_kernel(q_ref, k_ref, vt_ref, o_ref, *scratch):
    s_refs = scratch[0:B_NB]
    mx_refs = scratch[B_NB:2 * B_NB]
    m_ref, acc_ref = scratch[2 * B_NB:]
    n_q = q_ref.shape[1] // B_TQ

    def scores(blk, slot, q0, lo=0):
        start = pl.multiple_of(blk * B_TK, B_TK)
        k = k_ref[0, pl.ds(start, B_TK), :]
        q = q_ref[0, pl.ds(pl.multiple_of(q0 + lo, B_TK), B_TQ - lo), :]
        s = lax.dot_general(k, q, (((1,), (1,)), ((), ())), preferred_element_type=F32)
        s_refs[slot][:, lo:] = s
        mx_refs[slot][:, lo:] = jnp.max(s, axis=0, keepdims=True)

    def accumulate(blk, s, mx, lo=0):
        start = pl.multiple_of(blk * B_TK, B_TK)
        m_old = m_ref[:, lo:]
        m_new = jnp.maximum(m_old, mx)
        alpha = jnp.exp2(m_old - m_new)
        p = jnp.exp2(s - m_new).astype(BF16)
        vt = vt_ref[0, :, pl.ds(start, B_TK)]
        acc_ref[:, lo:] = alpha * acc_ref[:, lo:] + jnp.dot(vt, p, preferred_element_type=F32)
        m_ref[:, lo:] = m_new

    def diagonal_mask(s):
        kc = lax.broadcasted_iota(jnp.int32, s.shape, 0) // CHUNK
        qc = lax.broadcasted_iota(jnp.int32, s.shape, 1) // CHUNK
        return jnp.where(kc <= qc, s, NEG_INF)

    for u in range(B_AHEAD):
        scores(u, u, 0)

    def query_tile(qi, carry):
        q0 = qi * B_TQ
        m_ref[...] = jnp.full(m_ref.shape, NEG_INF, F32)
        acc_ref[...] = jnp.zeros(acc_ref.shape, F32)

        def steps(first_blk, n_blk):
            for u in range(n_blk):
                scores(first_blk + u + B_AHEAD, (u + B_AHEAD) % B_NB, q0)
                accumulate(first_blk + u, s_refs[u % B_NB][...], mx_refs[u % B_NB][...])

        def body(t, c):
            steps(t * 2 * B_NB, 2 * B_NB)
            return c

        lax.fori_loop(0, qi // 2, body, 0)

        @pl.when(qi % 2 == 1)
        def _():
            steps((qi - 1) * B_NB, B_NB)

        q0_next = jnp.minimum(qi + 1, n_q - 1) * B_TQ
        for u in range(B_NB):
            lo = u * B_TK
            if u + B_AHEAD < B_NB:
                scores(qi * B_NB + u + B_AHEAD, u + B_AHEAD, q0, lo=(u + B_AHEAD) * B_TK)
            else:
                scores(u + B_AHEAD - B_NB, u + B_AHEAD - B_NB, q0_next)
            s = diagonal_mask(s_refs[u][:, lo:])
            accumulate(qi * B_NB + u, s, jnp.max(s, axis=0, keepdims=True), lo=lo)
        denom = acc_ref[B_V_DIM:B_V_DIM + 1, :]
        o_ref[0, :, pl.ds(pl.multiple_of(q0, B_TQ), B_TQ)] = (acc_ref[0:B_V_DIM, :] * (1.0 / denom)).astype(BF16)
        return carry

    lax.fori_loop(0, n_q, query_tile, 0)


def _mixer_b(qb, kb, vbt):
    B, S, _ = qb.shape
    return pl.pallas_call(
        _mixer_b_kernel,
        grid=(B, B_HEADS),
        in_specs=[pl.BlockSpec((1, S, LANES), lambda b, h: (b, 0, h)),
                  pl.BlockSpec((1, S, LANES), lambda b, h: (b, 0, h)),
                  pl.BlockSpec((1, VT_ROWS, S), lambda b, h: (b, h, 0))],
        out_specs=pl.BlockSpec((1, B_V_DIM, S), lambda b, h: (b, h, 0)),
        out_shape=jax.ShapeDtypeStruct((B, B_WIDTH, S), BF16),
        scratch_shapes=[pltpu.VMEM((B_TK, B_TQ), F32)] * B_NB + [pltpu.VMEM((1, B_TQ), F32)] * B_NB
                       + [pltpu.VMEM((1, B_TQ), F32), pltpu.VMEM((VT_ROWS, B_TQ), F32)],
        compiler_params=pltpu.CompilerParams(dimension_semantics=("arbitrary", "arbitrary"),
                                             vmem_limit_bytes=VMEM_LIMIT),
        name="mixer_b",
    )(qb, kb, vbt)


def _merge_kernel(x_ref, yat_ref, ybt_ref, lng_ref, lnb_ref, wzg_ref, bzg_ref, wpa_ref, wpb_ref, wout_ref,
                  pg_ref, pb_ref, o_ref):
    rows = x_ref.shape[1] // MERGE_GROUPS
    for g in range(MERGE_GROUPS):
        r = slice(g * rows, (g + 1) * rows)
        h = _layer_norm(x_ref[0, r, :], lng_ref[...], lnb_ref[...])
        zg = jnp.dot(h.astype(BF16), wzg_ref[...], preferred_element_type=F32) + bzg_ref[...]
        za = zg[:, 0:A_WIDTH]
        zb = zg[:, A_WIDTH:A_WIDTH + B_WIDTH]
        ga = zg[:, A_WIDTH + B_WIDTH:A_WIDTH + B_WIDTH + D_MODEL]
        gb = zg[:, A_WIDTH + B_WIDTH + D_MODEL:]
        ya_in = yat_ref[0, :, r].astype(F32).T * (za * jax.nn.sigmoid(za))
        yb_in = ybt_ref[0, :, r].astype(F32).T * (zb * jax.nn.sigmoid(zb))
        ya = jnp.dot(ya_in.astype(BF16), wpa_ref[...], preferred_element_type=F32)
        yb = jnp.dot(yb_in.astype(BF16), wpb_ref[...], preferred_element_type=F32)
        mixed = jax.nn.sigmoid(ga) * ya + jax.nn.sigmoid(gb) * yb
        out = jnp.dot(mixed.astype(BF16), wout_ref[...], preferred_element_type=F32)
        o_ref[0, r, :] = _layer_norm(DEEPNORM_ALPHA * h + out, pg_ref[...], pb_ref[...])


def _merge(x, yat, ybt, lng, lnb, wzg, bzg, wpa, wpb, wout, pg, pb):
    B, S, _ = x.shape
    tm = OUT_TM
    row = lambda w: pl.BlockSpec((1, tm, w), lambda b, i: (b, i, 0))
    nzg = wzg.shape[1]
    return pl.pallas_call(
        _merge_kernel,
        grid=(B, S // tm),
        in_specs=[row(D_MODEL), pl.BlockSpec((1, A_WIDTH, tm), lambda b, i: (b, 0, i)),
                  pl.BlockSpec((1, B_WIDTH, tm), lambda b, i: (b, 0, i)),
                  _const_spec((1, D_MODEL)), _const_spec((1, D_MODEL)),
                  _const_spec((D_MODEL, nzg)), _const_spec((1, nzg)),
                  _const_spec((A_WIDTH, D_MODEL)), _const_spec((B_WIDTH, D_MODEL)),
                  _const_spec((D_MODEL, D_MODEL)), _const_spec((1, D_MODEL)), _const_spec((1, D_MODEL))],
        out_specs=row(D_MODEL),
        out_shape=jax.ShapeDtypeStruct((B, S, D_MODEL), F32),
        compiler_params=pltpu.CompilerParams(dimension_semantics=("arbitrary", "arbitrary"),
                                             vmem_limit_bytes=VMEM_LIMIT),
        name="merge_out",
    )(x, yat, ybt, lng, lnb, wzg, bzg, wpa, wpb, wout, pg, pb)


def _rot_cols(w):
    half = w.shape[-1] // 2
    return jnp.concatenate([-w[..., half:], w[..., :half]], axis=-1)


def _prep_layer(w_in, b_in, w_uq, w_ukv):
    c = 0
    cols = {}
    for name, width in (("aq", A_WIDTH), ("ak", A_WIDTH), ("av", A_WIDTH), ("az", A_WIDTH), ("cq", Q_LORA),
                        ("ckv", KV_LORA), ("kr", B_ROPE_DIM), ("bz", B_WIDTH), ("ga", D_MODEL), ("gb", D_MODEL)):
        cols[name] = slice(c, c + width)
        c += width
    w = lambda n: w_in[:, cols[n]]
    b = lambda n: b_in[cols[n]]
    a_scale = A_HEAD_DIM ** -0.5 * LOG2E
    zeros_w = jnp.zeros((D_MODEL, B_NOPE_DIM), F32)
    zeros_b = jnp.zeros((B_NOPE_DIM,), F32)
    w1 = jnp.concatenate([w("aq") * a_scale, w("ak"), w("cq"), w("ckv"),
                          zeros_w, w("kr"), w("kr"),
                          zeros_w, _rot_cols(w("kr")), _rot_cols(w("kr"))], axis=1)
    b1 = jnp.concatenate([b("aq") * a_scale, b("ak"), b("cq"), b("ckv"),
                          zeros_b, b("kr"), b("kr"),
                          zeros_b, _rot_cols(b("kr")), _rot_cols(b("kr"))])
    wzg = jnp.concatenate([w("az"), w("bz"), w("ga"), w("gb")], axis=1)
    bzg = jnp.concatenate([b("az"), b("bz"), b("ga"), b("gb")])

    uq = w_uq.reshape(Q_LORA, B_HEADS, B_QK_DIM)
    uq_rope = uq[:, :, B_NOPE_DIM:]
    wq = jnp.concatenate([uq, _rot_cols(uq_rope)], axis=-1).reshape(Q_LORA, B_HEADS * LANES)
    ukv = w_ukv.reshape(KV_LORA, B_HEADS, B_NOPE_DIM + B_V_DIM)
    wk = jnp.concatenate([ukv[:, :, :B_NOPE_DIM], jnp.zeros((KV_LORA, B_HEADS, LANES - B_NOPE_DIM), F32)],
                         axis=-1).reshape(KV_LORA, B_HEADS * LANES)
    wvt = ukv[:, :, B_NOPE_DIM:].reshape(KV_LORA, B_WIDTH).T
    return (w1.astype(BF16), b1[None, :], w("av").T.astype(BF16), b("av")[:, None], wq.astype(BF16),
            wk.astype(BF16), wvt.astype(BF16), wzg.astype(BF16), bzg[None, :])


def _rope_freq_row():
    half = B_ROPE_DIM // 2
    inv_freq = ROPE_THETA ** (-jnp.arange(half, dtype=F32) / half)
    return jnp.concatenate([jnp.zeros((B_NOPE_DIM,), F32), inv_freq, inv_freq, inv_freq, inv_freq])[None, :]


def kernel(x, positions, ln_in_g, ln_in_b, w_in, b_in, q_norm_g, kv_norm_g, w_uq, w_ukv, rel_bias, w_proj_a,
           w_proj_b, w_out, ln_post_g, ln_post_b):
    depth = w_in.shape[0]
    assert depth == 1, "the trunk-entry norm is recomputed per kernel, which is only valid for one layer"
    B, S, _ = x.shape
    pos3 = positions.reshape(B, 1, S)
    lng, lnb = ln_in_g[None, :], ln_in_b[None, :]
    freq = _rope_freq_row()
    l = 0
    w1, b1, wavt, bav, wq, wk, wvt, wzg, bzg = _prep_layer(w_in[l], b_in[l], w_uq[l], w_ukv[l])
    aq, ak, avt, qb, kb, vbt = _token_projections(x, pos3, lng, lnb, w1, b1, wavt, bav, q_norm_g[l][None, :],
                                                  kv_norm_g[l][None, :], wq, wk, wvt, freq)
    yat = _mixer_a(aq, ak, avt, _mixer_a_bias(rel_bias[l]))
    ybt = _mixer_b(qb, kb, vbt)
    return _merge(x, yat, ybt, lng, lnb, wzg, bzg, w_proj_a[l].astype(BF16), w_proj_b[l].astype(BF16),
                  w_out[l].astype(BF16), ln_post_g[l][None, :], ln_post_b[l][None, :])
```

```python
import jax
import jax.numpy as jnp
from jax import lax
from jax.experimental import pallas as pl
from jax.experimental.pallas import tpu as pltpu

D_MODEL = 1024
CHUNK = 64
A_HEADS = 8
A_HEAD_DIM = 64
A_WIDTH = A_HEADS * A_HEAD_DIM
A_LEFT_CHUNKS = 8
REL_CLIP = 128
B_HEADS = 8
B_NOPE_DIM = 64
B_ROPE_DIM = 32
B_QK_DIM = B_NOPE_DIM + B_ROPE_DIM
B_V_DIM = 64
B_WIDTH = B_HEADS * B_V_DIM
Q_LORA = 256
KV_LORA = 128
ROPE_THETA = 10000.0
DEEPNORM_ALPHA = 2.0 ** 0.25
LN_EPS = 1e-5
RMS_EPS = 1e-6
NEG_INF = -1e30

LANES = 128
VMEM_LIMIT = 56 * 1024 * 1024

PROJ_TM = 1024
A_TQ = 256
A_WIN = A_TQ + A_LEFT_CHUNKS * CHUNK
A_SUB = 2
A_STEP = A_SUB * A_TQ
A_ROLL = 1024
B_TQ = 1024
B_TK = 256
B_NB = B_TQ // B_TK
B_AHEAD = 2
V_DIM = 64
VT_ROWS = 80
LOG2E = 1.4426950408889634
OUT_TM = 512
MERGE_GROUPS = 2

BF16 = jnp.bfloat16
F32 = jnp.float32


def _layer_norm(x, g, b):
    mu = jnp.mean(x, axis=-1, keepdims=True)
    xc = x - mu
    var = jnp.mean(xc * xc, axis=-1, keepdims=True)
    return xc * lax.rsqrt(var + LN_EPS) * g + b


def _rms_norm(x, g):
    return x * lax.rsqrt(jnp.mean(x * x, axis=-1, keepdims=True) + RMS_EPS) * g


def _const_spec(shape):
    nd = len(shape)
    return pl.BlockSpec(shape, lambda *_: (0,) * nd, pipeline_mode=pl.Buffered(1))


def _store_values_t(ref, vt):
    tail = VT_ROWS - V_DIM
    ones_rows = (lax.broadcasted_iota(jnp.int32, (tail, vt.shape[1]), 0) == 0).astype(BF16)
    for hd in range(vt.shape[0] // V_DIM):
        ref[0, hd * VT_ROWS:hd * VT_ROWS + V_DIM, :] = vt[hd * V_DIM:(hd + 1) * V_DIM].astype(BF16)
        ref[0, hd * VT_ROWS + V_DIM:(hd + 1) * VT_ROWS, :] = ones_rows


def _proj_kernel(x_ref, pos_ref, lng_ref, lnb_ref, w1_ref, b1_ref, wavt_ref, bav_ref, qg_ref, kvg_ref, wq_ref,
                 wk_ref, wvt_ref, freq_ref, aq_ref, ak_ref, avt_ref, qb_ref, kb_ref, vbt_ref):
    hb = _layer_norm(x_ref[0], lng_ref[...], lnb_ref[...]).astype(BF16)
    proj = jnp.dot(hb, w1_ref[...], preferred_element_type=F32) + b1_ref[...]
    aq_ref[0] = proj[:, 0:A_WIDTH].astype(BF16)
    ak_ref[0] = proj[:, A_WIDTH:2 * A_WIDTH].astype(BF16)
    avt = lax.dot_general(wavt_ref[...], hb, (((1,), (1,)), ((), ())), preferred_element_type=F32) + bav_ref[...]
    _store_values_t(avt_ref, avt)
    o = 2 * A_WIDTH
    cq = _rms_norm(proj[:, o:o + Q_LORA], qg_ref[...]).astype(BF16)
    o += Q_LORA
    ckv = _rms_norm(proj[:, o:o + KV_LORA], kvg_ref[...]).astype(BF16)
    o += KV_LORA
    kr = proj[:, o:o + LANES]

    pos_rows = jnp.broadcast_to(pos_ref[0].astype(F32), (LANES, pos_ref.shape[2])).T
    ang = pos_rows * freq_ref[...]
    cos_t = jnp.cos(ang)
    sin_t = jnp.sin(ang)
    lane = lax.broadcasted_iota(jnp.int32, (1, LANES), 1)
    scale = B_QK_DIM ** -0.5 * LOG2E
    tq = jnp.where(lane < B_NOPE_DIM, scale, jnp.where(lane < B_QK_DIM, cos_t, sin_t) * scale)
    prod = kr * jnp.where(lane < B_NOPE_DIM, cos_t, sin_t)
    k_rope = jnp.where(lane >= B_NOPE_DIM, prod + pltpu.roll(prod, LANES // 2, 1), 0.0)

    q_all = jnp.dot(cq, wq_ref[...], preferred_element_type=F32)
    k_all = jnp.dot(ckv, wk_ref[...], preferred_element_type=F32)
    for hd in range(B_HEADS):
        sl = slice(hd * LANES, (hd + 1) * LANES)
        qb_ref[0, :, sl] = (q_all[:, sl] * tq).astype(BF16)
        kb_ref[0, :, sl] = (k_all[:, sl] + k_rope).astype(BF16)
    vt = lax.dot_general(wvt_ref[...], ckv, (((1,), (1,)), ((), ())), preferred_element_type=F32)
    _store_values_t(vbt_ref, vt)


def _token_projections(x, pos3, lng, lnb, w1, b1, wavt, bav, qg, kvg, wq, wk, wvt, freq):
    B, S, _ = x.shape
    tm = PROJ_TM
    n1 = w1.shape[1]
    row = lambda w: pl.BlockSpec((1, tm, w), lambda b, i: (b, i, 0))
    vt_spec = pl.BlockSpec((1, B_HEADS * VT_ROWS, tm), lambda b, i: (b, 0, i))
    vt_shape = jax.ShapeDtypeStruct((B, B_HEADS * VT_ROWS, S), BF16)
    return pl.pallas_call(
        _proj_kernel,
        grid=(B, S // tm),
        in_specs=[row(D_MODEL), pl.BlockSpec((1, 1, tm), lambda b, i: (b, 0, i)),
                  _const_spec((1, D_MODEL)), _const_spec((1, D_MODEL)),
                  _const_spec((D_MODEL, n1)), _const_spec((1, n1)),
                  _const_spec((A_WIDTH, D_MODEL)), _const_spec((A_WIDTH, 1)),
                  _const_spec((1, Q_LORA)), _const_spec((1, KV_LORA)),
                  _const_spec((Q_LORA, B_HEADS * LANES)), _const_spec((KV_LORA, B_HEADS * LANES)),
                  _const_spec((B_WIDTH, KV_LORA)), _const_spec((1, LANES))],
        out_specs=[row(A_WIDTH), row(A_WIDTH), vt_spec, row(B_HEADS * LANES), row(B_HEADS * LANES), vt_spec],
        out_shape=[jax.ShapeDtypeStruct((B, S, A_WIDTH), BF16)] * 2 + [vt_shape]
                  + [jax.ShapeDtypeStruct((B, S, B_HEADS * LANES), BF16)] * 2 + [vt_shape],
        compiler_params=pltpu.CompilerParams(dimension_semantics=("arbitrary", "arbitrary"),
                                             vmem_limit_bytes=VMEM_LIMIT),
        name="token_projections",
    )(x, pos3, lng, lnb, w1, b1, wavt, bav, qg, kvg, wq, wk, wvt, freq)


def _mixer_a_kernel(q_ref, k1_ref, k0_ref, v1_ref, v0_ref, bias_ref, o_ref, s0_ref, s1_ref, mx0_ref, mx1_ref):
    i = pl.program_id(1)
    s_refs, mx_refs = (s0_ref, s1_ref), (mx0_ref, mx1_ref)
    lane = lax.broadcasted_iota(jnp.int32, (1, LANES), 1)

    def scores(unit, slot):
        t, pair = divmod(unit, A_HEADS // 2)
        sl = slice(pair * LANES, (pair + 1) * LANES)
        qp = q_ref[0, t * A_TQ:(t + 1) * A_TQ, sl]
        kwin = jnp.concatenate([k1_ref[0, :, sl], k0_ref[0, :, sl]], axis=0)
        kp = kwin[t * A_TQ:t * A_TQ + A_WIN]
        q2 = jnp.concatenate([jnp.where((lane // A_HEAD_DIM) == hh, qp, jnp.zeros_like(qp)) for hh in range(2)],
                             axis=0)
        s = lax.dot_general(kp, q2, (((1,), (1,)), ((), ())), preferred_element_type=F32)
        variant = jnp.where(i == 0, t + 1, 0)
        s = s + jnp.concatenate([bias_ref[variant, 2 * pair], bias_ref[variant, 2 * pair + 1]], axis=1)
        s_refs[slot][...] = s
        mx_refs[slot][...] = jnp.max(s, axis=0, keepdims=True)

    def finish(unit, slot):
        t, pair = divmod(unit, A_HEADS // 2)
        p = jnp.exp2(s_refs[slot][...] - mx_refs[slot][...]).astype(BF16)
        rows = slice(2 * pair * VT_ROWS, (2 * pair + 2) * VT_ROWS)
        vwin = jnp.concatenate([v1_ref[0, rows, :], v0_ref[0, rows, :]], axis=1)
        vt = vwin[:, t * A_TQ:t * A_TQ + A_WIN]
        for hh in range(2):
            oh = jnp.dot(vt[hh * VT_ROWS:(hh + 1) * VT_ROWS], p[:, hh * A_TQ:(hh + 1) * A_TQ],
                         preferred_element_type=F32)
            hd = 2 * pair + hh
            o_ref[0, hd * V_DIM:(hd + 1) * V_DIM, t * A_TQ:(t + 1) * A_TQ] = (
                oh[:V_DIM] * (1.0 / oh[V_DIM:V_DIM + 1])).astype(BF16)

    n_units = A_SUB * A_HEADS // 2
    scores(0, 0)
    for unit in range(n_units):
        if unit + 1 < n_units:
            scores(unit + 1, (unit + 1) % 2)
        finish(unit, unit % 2)


def _mixer_a(aq, ak, avt, bias):
    B, S, W = aq.shape
    blk = lambda back: pl.BlockSpec((1, A_STEP, W), lambda b, i: (b, jnp.maximum(i - back, 0), 0))
    vblk = lambda back: pl.BlockSpec((1, A_HEADS * VT_ROWS, A_STEP), lambda b, i: (b, 0, jnp.maximum(i - back, 0)))
    return pl.pallas_call(
        _mixer_a_kernel,
        grid=(B, S // A_STEP),
        in_specs=[blk(0), blk(1), blk(0), vblk(1), vblk(0), _const_spec(bias.shape)],
        out_specs=pl.BlockSpec((1, W, A_STEP), lambda b, i: (b, 0, i)),
        out_shape=jax.ShapeDtypeStruct((B, W, S), BF16),
        scratch_shapes=[pltpu.VMEM((A_WIN, 2 * A_TQ), F32)] * 2 + [pltpu.VMEM((1, 2 * A_TQ), F32)] * 2,
        compiler_params=pltpu.CompilerParams(dimension_semantics=("arbitrary", "arbitrary"),
                                             vmem_limit_bytes=VMEM_LIMIT),
        name="mixer_a",
    )(aq, ak, ak, avt, avt, bias)


def _bias_kernel(row_ref, o_ref):
    rows = jnp.broadcast_to(row_ref[0], (A_WIN, A_ROLL))
    table = pltpu.roll(rows, 0, 1, stride=1, stride_axis=0)[:, :A_TQ]
    kc = lax.broadcasted_iota(jnp.int32, (A_WIN, A_TQ), 0) // CHUNK
    qc = lax.broadcasted_iota(jnp.int32, (A_WIN, A_TQ), 1) // CHUNK
    gap = qc + A_LEFT_CHUNKS - kc
    v = pl.program_id(0)
    n_pad = jnp.where(v == 0, 0, A_LEFT_CHUNKS * CHUNK - (v - 1) * A_TQ)
    key_row = lax.broadcasted_iota(jnp.int32, (A_WIN, A_TQ), 0)
    visible = (gap >= 0) & (gap <= A_LEFT_CHUNKS) & (key_row >= n_pad)
    o_ref[0, 0] = jnp.where(visible, table * LOG2E, NEG_INF)


def _mixer_a_bias(rel_bias):
    tbl = rel_bias.T.astype(F32)
    n_old = A_LEFT_CHUNKS * CHUNK
    first_tbl = A_ROLL - n_old - REL_CLIP
    far = jnp.broadcast_to(tbl[:, -1:], (A_HEADS, A_ROLL))
    near = jnp.broadcast_to(tbl[:, :1], (A_HEADS, A_ROLL))
    row = jnp.concatenate([far[:, :A_ROLL - A_WIN + 1], near[:, A_ROLL - A_WIN + 1:first_tbl], tbl,
                           far[:, first_tbl + tbl.shape[1]:]], axis=1)
    row = row[:, None, :]
    return pl.pallas_call(
        _bias_kernel,
        grid=(1 + A_SUB, A_HEADS),
        in_specs=[pl.BlockSpec((1, 1, A_ROLL), lambda v, h: (h, 0, 0))],
        out_specs=pl.BlockSpec((1, 1, A_WIN, A_TQ), lambda v, h: (v, h, 0, 0)),
        out_shape=jax.ShapeDtypeStruct((1 + A_SUB, A_HEADS, A_WIN, A_TQ), F32),
        name="mixer_a_bias",
    )(row)


def _mixer_b_kernel(q_ref, k_ref, vt_ref, o_ref, *scratch):
    s_refs = scratch[0:B_NB]
    mx_refs = scratch[B_NB:2 * B_NB]
    m_ref, acc_ref = scratch[2 * B_NB:]
    n_q = q_ref.shape[1] // B_TQ

    def scores(blk, slot, q0, lo=0):
        start = pl.multiple_of(blk * B_TK, B_TK)
        k = k_ref[0, pl.ds(start, B_TK), :]
        q = q_ref[0, pl.ds(pl.multiple_of(q0 + lo, B_TK), B_TQ - lo), :]
        s = lax.dot_general(k, q, (((1,), (1,)), ((), ())), preferred_element_type=F32)
        s_refs[slot][:, lo:] = s
        mx_refs[slot][:, lo:] = jnp.max(s, axis=0, keepdims=True)

    def accumulate(blk, s, mx, lo=0):
        start = pl.multiple_of(blk * B_TK, B_TK)
        m_old = m_ref[:, lo:]
        m_new = jnp.maximum(m_old, mx)
        alpha = jnp.exp2(m_old - m_new)
        p = jnp.exp2(s - m_new).astype(BF16)
        vt = vt_ref[0, :, pl.ds(start, B_TK)]
        acc_ref[:, lo:] = alpha * acc_ref[:, lo:] + jnp.dot(vt, p, preferred_element_type=F32)
        m_ref[:, lo:] = m_new

    def diagonal_mask(s):
        kc = lax.broadcasted_iota(jnp.int32, s.shape, 0) // CHUNK
        qc = lax.broadcasted_iota(jnp.int32, s.shape, 1) // CHUNK
        return jnp.where(kc <= qc, s, NEG_INF)

    for u in range(B_AHEAD):
        scores(u, u, 0)

    def query_tile(qi, carry):
        q0 = qi * B_TQ
        m_ref[...] = jnp.full(m_ref.shape, NEG_INF, F32)
        acc_ref[...] = jnp.zeros(acc_ref.shape, F32)

        def steps(first_blk, n_blk):
            for u in range(n_blk):
                scores(first_blk + u + B_AHEAD, (u + B_AHEAD) % B_NB, q0)
                accumulate(first_blk + u, s_refs[u % B_NB][...], mx_refs[u % B_NB][...])

        def body(t, c):
            steps(t * 2 * B_NB, 2 * B_NB)
            return c

        lax.fori_loop(0, qi // 2, body, 0)

        @pl.when(qi % 2 == 1)
        def _():
            steps((qi - 1) * B_NB, B_NB)

        q0_next = jnp.minimum(qi + 1, n_q - 1) * B_TQ
        for u in range(B_NB):
            lo = u * B_TK
            if u + B_AHEAD < B_NB:
                scores(qi * B_NB + u + B_AHEAD, u + B_AHEAD, q0, lo=(u + B_AHEAD) * B_TK)
            else:
                scores(u + B_AHEAD - B_NB, u + B_AHEAD - B_NB, q0_next)
            s = diagonal_mask(s_refs[u][:, lo:])
            accumulate(qi * B_NB + u, s, jnp.max(s, axis=0, keepdims=True), lo=lo)
        denom = acc_ref[B_V_DIM:B_V_DIM + 1, :]
        o_ref[0, :, pl.ds(pl.multiple_of(q0, B_TQ), B_TQ)] = (acc_ref[0:B_V_DIM, :] * (1.0 / denom)).astype(BF16)
        return carry

    lax.fori_loop(0, n_q, query_tile, 0)


def _mixer_b(qb, kb, vbt):
    B, S, _ = qb.shape
    return pl.pallas_call(
        _mixer_b_kernel,
        grid=(B, B_HEADS),
        in_specs=[pl.BlockSpec((1, S, LANES), lambda b, h: (b, 0, h)),
                  pl.BlockSpec((1, S, LANES), lambda b, h: (b, 0, h)),
                  pl.BlockSpec((1, VT_ROWS, S), lambda b, h: (b, h, 0))],
        out_specs=pl.BlockSpec((1, B_V_DIM, S), lambda b, h: (b, h, 0)),
        out_shape=jax.ShapeDtypeStruct((B, B_WIDTH, S), BF16),
        scratch_shapes=[pltpu.VMEM((B_TK, B_TQ), F32)] * B_NB + [pltpu.VMEM((1, B_TQ), F32)] * B_NB
                       + [pltpu.VMEM((1, B_TQ), F32), pltpu.VMEM((VT_ROWS, B_TQ), F32)],
        compiler_params=pltpu.CompilerParams(dimension_semantics=("arbitrary", "arbitrary"),
                                             vmem_limit_bytes=VMEM_LIMIT),
        name="mixer_b",
    )(qb, kb, vbt)


def _merge_kernel(x_ref, yat_ref, ybt_ref, lng_ref, lnb_ref, wzg_ref, bzg_ref, wpa_ref, wpb_ref, wout_ref,
                  pg_ref, pb_ref, o_ref):
    rows = x_ref.shape[1] // MERGE_GROUPS
    for g in range(MERGE_GROUPS):
        r = slice(g * rows, (g + 1) * rows)
        h = _layer_norm(x_ref[0, r, :], lng_ref[...], lnb_ref[...])
        zg = jnp.dot(h.astype(BF16), wzg_ref[...], preferred_element_type=F32) + bzg_ref[...]
        za = zg[:, 0:A_WIDTH]
        zb = zg[:, A_WIDTH:A_WIDTH + B_WIDTH]
        ga = zg[:, A_WIDTH + B_WIDTH:A_WIDTH + B_WIDTH + D_MODEL]
        gb = zg[:, A_WIDTH + B_WIDTH + D_MODEL:]
        ya_in = yat_ref[0, :, r].astype(F32).T * (za * jax.nn.sigmoid(za))
        yb_in = ybt_ref[0, :, r].astype(F32).T * (zb * jax.nn.sigmoid(zb))
        ya = jnp.dot(ya_in.astype(BF16), wpa_ref[...], preferred_element_type=F32)
        yb = jnp.dot(yb_in.astype(BF16), wpb_ref[...], preferred_element_type=F32)
        mixed = jax.nn.sigmoid(ga) * ya + jax.nn.sigmoid(gb) * yb
        out = jnp.dot(mixed.astype(BF16), wout_ref[...], preferred_element_type=F32)
        o_ref[0, r, :] = _layer_norm(DEEPNORM_ALPHA * h + out, pg_ref[...], pb_ref[...])


def _merge(x, yat, ybt, lng, lnb, wzg, bzg, wpa, wpb, wout, pg, pb):
    B, S, _ = x.shape
    tm = OUT_TM
    row = lambda w: pl.BlockSpec((1, tm, w), lambda b, i: (b, i, 0))
    nzg = wzg.shape[1]
    return pl.pallas_call(
        _merge_kernel,
        grid=(B, S // tm),
        in_specs=[row(D_MODEL), pl.BlockSpec((1, A_WIDTH, tm), lambda b, i: (b, 0, i)),
                  pl.BlockSpec((1, B_WIDTH, tm), lambda b, i: (b, 0, i)),
                  _const_spec((1, D_MODEL)), _const_spec((1, D_MODEL)),
                  _const_spec((D_MODEL, nzg)), _const_spec((1, nzg)),
                  _const_spec((A_WIDTH, D_MODEL)), _const_spec((B_WIDTH, D_MODEL)),
                  _const_spec((D_MODEL, D_MODEL)), _const_spec((1, D_MODEL)), _const_spec((1, D_MODEL))],
        out_specs=row(D_MODEL),
        out_shape=jax.ShapeDtypeStruct((B, S, D_MODEL), F32),
        compiler_params=pltpu.CompilerParams(dimension_semantics=("arbitrary", "arbitrary"),
                                             vmem_limit_bytes=VMEM_LIMIT),
        name="merge_out",
    )(x, yat, ybt, lng, lnb, wzg, bzg, wpa, wpb, wout, pg, pb)


def _rot_cols(w):
    half = w.shape[-1] // 2
    return jnp.concatenate([-w[..., half:], w[..., :half]], axis=-1)


def _prep_layer(w_in, b_in, w_uq, w_ukv):
    c = 0
    cols = {}
    for name, width in (("aq", A_WIDTH), ("ak", A_WIDTH), ("av", A_WIDTH), ("az", A_WIDTH), ("cq", Q_LORA),
                        ("ckv", KV_LORA), ("kr", B_ROPE_DIM), ("bz", B_WIDTH), ("ga", D_MODEL), ("gb", D_MODEL)):
        cols[name] = slice(c, c + width)
        c += width
    w = lambda n: w_in[:, cols[n]]
    b = lambda n: b_in[cols[n]]
    a_scale = A_HEAD_DIM ** -0.5 * LOG2E
    w1 = jnp.concatenate([w("aq") * a_scale, w("ak"), w("cq"), w("ckv"),
                          w("kr"), w("kr"), _rot_cols(w("kr")), _rot_cols(w("kr"))], axis=1)
    b1 = jnp.concatenate([b("aq") * a_scale, b("ak"), b("cq"), b("ckv"),
                          b("kr"), b("kr"), _rot_cols(b("kr")), _rot_cols(b("kr"))])
    wzg = jnp.concatenate([w("az"), w("bz"), w("ga"), w("gb")], axis=1)
    bzg = jnp.concatenate([b("az"), b("bz"), b("ga"), b("gb")])

    uq = w_uq.reshape(Q_LORA, B_HEADS, B_QK_DIM)
    uq_rope = uq[:, :, B_NOPE_DIM:]
    wq = jnp.concatenate([uq, _rot_cols(uq_rope)], axis=-1).reshape(Q_LORA, B_HEADS * LANES)
    ukv = w_ukv.reshape(KV_LORA, B_HEADS, B_NOPE_DIM + B_V_DIM)
    wk = jnp.concatenate([ukv[:, :, :B_NOPE_DIM], jnp.zeros((KV_LORA, B_HEADS, LANES - B_NOPE_DIM), F32)],
                         axis=-1).reshape(KV_LORA, B_HEADS * LANES)
    wvt = ukv[:, :, B_NOPE_DIM:].reshape(KV_LORA, B_WIDTH).T
    return (w1.astype(BF16), b1[None, :], w("av").T.astype(BF16), b("av")[:, None], wq.astype(BF16),
            wk.astype(BF16), wvt.astype(BF16), wzg.astype(BF16), bzg[None, :])


def _rope_freq_row():
    half = B_ROPE_DIM // 2
    inv_freq = ROPE_THETA ** (-jnp.arange(half, dtype=F32) / half)
    return jnp.tile(inv_freq, LANES // half)[None, :]


def kernel(x, positions, ln_in_g, ln_in_b, w_in, b_in, q_norm_g, kv_norm_g, w_uq, w_ukv, rel_bias, w_proj_a,
           w_proj_b, w_out, ln_post_g, ln_post_b):
    depth = w_in.shape[0]
    assert depth == 1, "the trunk-entry norm is recomputed per kernel, which is only valid for one layer"
    B, S, _ = x.shape
    pos3 = positions.reshape(B, 1, S)
    lng, lnb = ln_in_g[None, :], ln_in_b[None, :]
    freq = _rope_freq_row()
    l = 0
    w1, b1, wavt, bav, wq, wk, wvt, wzg, bzg = _prep_layer(w_in[l], b_in[l], w_uq[l], w_ukv[l])
    aq, ak, avt, qb, kb, vbt = _token_projections(x, pos3, lng, lnb, w1, b1, wavt, bav, q_norm_g[l][None, :],
                                                  kv_norm_g[l][None, :], wq, wk, wvt, freq)
    yat = _mixer_a(aq, ak, avt, _mixer_a_bias(rel_bias[l]))
    ybt = _mixer_b(qb, kb, vbt)
    return _merge(x, yat, ybt, lng, lnb, wzg, bzg, w_proj_a[l].astype(BF16), w_proj_b[l].astype(BF16),
                  w_out[l].astype(BF16), ln_post_g[l][None, :], ln_post_b[l][None, :])
```

```python
import jax
import jax.numpy as jnp
from jax import lax
from jax.experimental import pallas as pl
from jax.experimental.pallas import tpu as pltpu

D_MODEL = 1024
CHUNK = 64
A_HEADS = 8
A_HEAD_DIM = 64
A_WIDTH = A_HEADS * A_HEAD_DIM
A_LEFT_CHUNKS = 8
REL_CLIP = 128
B_HEADS = 8
B_NOPE_DIM = 64
B_ROPE_DIM = 32
B_QK_DIM = B_NOPE_DIM + B_ROPE_DIM
B_V_DIM = 64
B_WIDTH = B_HEADS * B_V_DIM
Q_LORA = 256
KV_LORA = 128
ROPE_THETA = 10000.0
DEEPNORM_ALPHA = 2.0 ** 0.25
LN_EPS = 1e-5
RMS_EPS = 1e-6
NEG_INF = -1e30

LANES = 128
VMEM_LIMIT = 56 * 1024 * 1024

PROJ_TM = 1024
A_TQ = 256
A_WIN = A_TQ + A_LEFT_CHUNKS * CHUNK
A_SUB = 2
A_STEP = A_SUB * A_TQ
A_ROLL = 1024
B_TQ = 1024
B_TK = 256
B_NB = B_TQ // B_TK
B_AHEAD = 2
V_DIM = 64
VT_ROWS = 80
LOG2E = 1.4426950408889634
OUT_TM = 512
MERGE_GROUPS = 2

BF16 = jnp.bfloat16
F32 = jnp.float32


def _layer_norm(x, g, b):
    mu = jnp.mean(x, axis=-1, keepdims=True)
    xc = x - mu
    var = jnp.mean(xc * xc, axis=-1, keepdims=True)
    return xc * lax.rsqrt(var + LN_EPS) * g + b


def _rms_norm(x, g):
    return x * lax.rsqrt(jnp.mean(x * x, axis=-1, keepdims=True) + RMS_EPS) * g


def _const_spec(shape):
    nd = len(shape)
    return pl.BlockSpec(shape, lambda *_: (0,) * nd, pipeline_mode=pl.Buffered(1))


def _store_values_t(ref, vt):
    tail = VT_ROWS - V_DIM
    ones_rows = (lax.broadcasted_iota(jnp.int32, (tail, vt.shape[1]), 0) == 0).astype(BF16)
    for hd in range(vt.shape[0] // V_DIM):
        ref[0, hd * VT_ROWS:hd * VT_ROWS + V_DIM, :] = vt[hd * V_DIM:(hd + 1) * V_DIM].astype(BF16)
        ref[0, hd * VT_ROWS + V_DIM:(hd + 1) * VT_ROWS, :] = ones_rows


def _proj_kernel(x_ref, pos_ref, lng_ref, lnb_ref, w1_ref, b1_ref, wavt_ref, bav_ref, qg_ref, kvg_ref, wq_ref,
                 wk_ref, wvt_ref, freq_ref, aq_ref, ak_ref, avt_ref, qb_ref, kb_ref, vbt_ref):
    hb = _layer_norm(x_ref[0], lng_ref[...], lnb_ref[...]).astype(BF16)
    proj = jnp.dot(hb, w1_ref[...], preferred_element_type=F32) + b1_ref[...]
    aq_ref[0] = proj[:, 0:A_WIDTH].astype(BF16)
    ak_ref[0] = proj[:, A_WIDTH:2 * A_WIDTH].astype(BF16)
    avt = lax.dot_general(wavt_ref[...], hb, (((1,), (1,)), ((), ())), preferred_element_type=F32) + bav_ref[...]
    _store_values_t(avt_ref, avt)
    o = 2 * A_WIDTH
    cq = _rms_norm(proj[:, o:o + Q_LORA], qg_ref[...]).astype(BF16)
    o += Q_LORA
    ckv = _rms_norm(proj[:, o:o + KV_LORA], kvg_ref[...]).astype(BF16)
    o += KV_LORA
    kr = proj[:, o:o + LANES]

    pos_rows = jnp.broadcast_to(pos_ref[0].astype(F32), (LANES, pos_ref.shape[2])).T
    ang = pos_rows * freq_ref[...]
    cos_t = jnp.cos(ang)
    sin_t = jnp.sin(ang)
    lane = lax.broadcasted_iota(jnp.int32, (1, LANES), 1)
    scale = B_QK_DIM ** -0.5 * LOG2E
    tq = jnp.where(lane < B_NOPE_DIM, scale, jnp.where(lane < B_QK_DIM, cos_t, sin_t) * scale)
    prod = kr * jnp.where(lane < B_NOPE_DIM, cos_t, sin_t)
    k_rope = jnp.where(lane >= B_NOPE_DIM, prod + pltpu.roll(prod, LANES // 2, 1), 0.0)

    q_all = jnp.dot(cq, wq_ref[...], preferred_element_type=F32)
    k_all = jnp.dot(ckv, wk_ref[...], preferred_element_type=F32)
    for hd in range(B_HEADS):
        sl = slice(hd * LANES, (hd + 1) * LANES)
        qb_ref[0, :, sl] = (q_all[:, sl] * tq).astype(BF16)
        kb_ref[0, :, sl] = (k_all[:, sl] + k_rope).astype(BF16)
    vt = lax.dot_general(wvt_ref[...], ckv, (((1,), (1,)), ((), ())), preferred_element_type=F32)
    _store_values_t(vbt_ref, vt)


def _token_projections(x, pos3, lng, lnb, w1, b1, wavt, bav, qg, kvg, wq, wk, wvt, freq):
    B, S, _ = x.shape
    tm = PROJ_TM
    n1 = w1.shape[1]
    row = lambda w: pl.BlockSpec((1, tm, w), lambda b, i: (b, i, 0))
    vt_spec = pl.BlockSpec((1, B_HEADS * VT_ROWS, tm), lambda b, i: (b, 0, i))
    vt_shape = jax.ShapeDtypeStruct((B, B_HEADS * VT_ROWS, S), BF16)
    return pl.pallas_call(
        _proj_kernel,
        grid=(B, S // tm),
        in_specs=[row(D_MODEL), pl.BlockSpec((1, 1, tm), lambda b, i: (b, 0, i)),
                  _const_spec((1, D_MODEL)), _const_spec((1, D_MODEL)),
                  _const_spec((D_MODEL, n1)), _const_spec((1, n1)),
                  _const_spec((A_WIDTH, D_MODEL)), _const_spec((A_WIDTH, 1)),
                  _const_spec((1, Q_LORA)), _const_spec((1, KV_LORA)),
                  _const_spec((Q_LORA, B_HEADS * LANES)), _const_spec((KV_LORA, B_HEADS * LANES)),
                  _const_spec((B_WIDTH, KV_LORA)), _const_spec((1, LANES))],
        out_specs=[row(A_WIDTH), row(A_WIDTH), vt_spec, row(B_HEADS * LANES), row(B_HEADS * LANES), vt_spec],
        out_shape=[jax.ShapeDtypeStruct((B, S, A_WIDTH), BF16)] * 2 + [vt_shape]
                  + [jax.ShapeDtypeStruct((B, S, B_HEADS * LANES), BF16)] * 2 + [vt_shape],
        compiler_params=pltpu.CompilerParams(dimension_semantics=("arbitrary", "arbitrary"),
                                             vmem_limit_bytes=VMEM_LIMIT),
        name="token_projections",
    )(x, pos3, lng, lnb, w1, b1, wavt, bav, qg, kvg, wq, wk, wvt, freq)


def _mixer_a_kernel(q_ref, k1_ref, k0_ref, v1_ref, v0_ref, bias_ref, o_ref, s0_ref, s1_ref, mx0_ref, mx1_ref):
    i = pl.program_id(1)
    s_refs, mx_refs = (s0_ref, s1_ref), (mx0_ref, mx1_ref)
    n_old = A_LEFT_CHUNKS * CHUNK
    lane = lax.broadcasted_iota(jnp.int32, (1, LANES), 1)
    row = lax.broadcasted_iota(jnp.int32, (n_old, 1), 0)

    def scores(unit, slot):
        t, pair = divmod(unit, A_HEADS // 2)
        sl = slice(pair * LANES, (pair + 1) * LANES)
        qp = q_ref[0, t * A_TQ:(t + 1) * A_TQ, sl]
        kwin = jnp.concatenate([k1_ref[0, :, sl], k0_ref[0, :, sl]], axis=0)
        kp = kwin[t * A_TQ:t * A_TQ + A_WIN]
        q2 = jnp.concatenate([jnp.where((lane // A_HEAD_DIM) == hh, qp, jnp.zeros_like(qp)) for hh in range(2)],
                             axis=0)
        s = lax.dot_general(kp, q2, (((1,), (1,)), ((), ())), preferred_element_type=F32)
        s = s + jnp.concatenate([bias_ref[2 * pair], bias_ref[2 * pair + 1]], axis=1)
        kpos = row + (i * A_STEP + t * A_TQ - n_old)
        pad_mask = jnp.where(kpos >= 0, 0.0, NEG_INF).astype(F32)
        s = jnp.concatenate([s[:n_old] + pad_mask, s[n_old:]], axis=0)
        s_refs[slot][...] = s
        mx_refs[slot][...] = jnp.max(s, axis=0, keepdims=True)

    def finish(unit, slot):
        t, pair = divmod(unit, A_HEADS // 2)
        p = jnp.exp2(s_refs[slot][...] - mx_refs[slot][...]).astype(BF16)
        rows = slice(2 * pair * VT_ROWS, (2 * pair + 2) * VT_ROWS)
        vwin = jnp.concatenate([v1_ref[0, rows, :], v0_ref[0, rows, :]], axis=1)
        vt = vwin[:, t * A_TQ:t * A_TQ + A_WIN]
        for hh in range(2):
            oh = jnp.dot(vt[hh * VT_ROWS:(hh + 1) * VT_ROWS], p[:, hh * A_TQ:(hh + 1) * A_TQ],
                         preferred_element_type=F32)
            hd = 2 * pair + hh
            o_ref[0, hd * V_DIM:(hd + 1) * V_DIM, t * A_TQ:(t + 1) * A_TQ] = (
                oh[:V_DIM] * (1.0 / oh[V_DIM:V_DIM + 1])).astype(BF16)

    n_units = A_SUB * A_HEADS // 2
    scores(0, 0)
    for unit in range(n_units):
        if unit + 1 < n_units:
            scores(unit + 1, (unit + 1) % 2)
        finish(unit, unit % 2)


def _mixer_a(aq, ak, avt, bias):
    B, S, W = aq.shape
    blk = lambda back: pl.BlockSpec((1, A_STEP, W), lambda b, i: (b, jnp.maximum(i - back, 0), 0))
    vblk = lambda back: pl.BlockSpec((1, A_HEADS * VT_ROWS, A_STEP), lambda b, i: (b, 0, jnp.maximum(i - back, 0)))
    return pl.pallas_call(
        _mixer_a_kernel,
        grid=(B, S // A_STEP),
        in_specs=[blk(0), blk(1), blk(0), vblk(1), vblk(0), _const_spec(bias.shape)],
        out_specs=pl.BlockSpec((1, W, A_STEP), lambda b, i: (b, 0, i)),
        out_shape=jax.ShapeDtypeStruct((B, W, S), BF16),
        scratch_shapes=[pltpu.VMEM((A_WIN, 2 * A_TQ), F32)] * 2 + [pltpu.VMEM((1, 2 * A_TQ), F32)] * 2,
        compiler_params=pltpu.CompilerParams(dimension_semantics=("arbitrary", "arbitrary"),
                                             vmem_limit_bytes=VMEM_LIMIT),
        name="mixer_a",
    )(aq, ak, ak, avt, avt, bias)


def _bias_kernel(row_ref, o_ref):
    rows = jnp.broadcast_to(row_ref[0], (A_WIN, A_ROLL))
    table = pltpu.roll(rows, 0, 1, stride=1, stride_axis=0)[:, :A_TQ]
    kc = lax.broadcasted_iota(jnp.int32, (A_WIN, A_TQ), 0) // CHUNK
    qc = lax.broadcasted_iota(jnp.int32, (A_WIN, A_TQ), 1) // CHUNK
    gap = qc + A_LEFT_CHUNKS - kc
    o_ref[0] = jnp.where((gap >= 0) & (gap <= A_LEFT_CHUNKS), table * LOG2E, NEG_INF)


def _mixer_a_bias(rel_bias):
    tbl = rel_bias.T.astype(F32)
    n_old = A_LEFT_CHUNKS * CHUNK
    first_tbl = A_ROLL - n_old - REL_CLIP
    far = jnp.broadcast_to(tbl[:, -1:], (A_HEADS, A_ROLL))
    near = jnp.broadcast_to(tbl[:, :1], (A_HEADS, A_ROLL))
    row = jnp.concatenate([far[:, :A_ROLL - A_WIN + 1], near[:, A_ROLL - A_WIN + 1:first_tbl], tbl,
                           far[:, first_tbl + tbl.shape[1]:]], axis=1)
    row = row[:, None, :]
    return pl.pallas_call(
        _bias_kernel,
        grid=(A_HEADS,),
        in_specs=[pl.BlockSpec((1, 1, A_ROLL), lambda h: (h, 0, 0))],
        out_specs=pl.BlockSpec((1, A_WIN, A_TQ), lambda h: (h, 0, 0)),
        out_shape=jax.ShapeDtypeStruct((A_HEADS, A_WIN, A_TQ), F32),
        name="mixer_a_bias",
    )(row)


def _mixer_b_kernel(q_ref, k_ref, vt_ref, o_ref, *scratch):
    s_refs = scratch[0:B_NB]
    mx_refs = scratch[B_NB:2 * B_NB]
    m_ref, acc_ref = scratch[2 * B_NB:]
    n_q = q_ref.shape[1] // B_TQ

    def scores(blk, slot, q0, lo=0):
        start = pl.multiple_of(blk * B_TK, B_TK)
        k = k_ref[0, pl.ds(start, B_TK), :]
        q = q_ref[0, pl.ds(pl.multiple_of(q0 + lo, B_TK), B_TQ - lo), :]
        s = lax.dot_general(k, q, (((1,), (1,)), ((), ())), preferred_element_type=F32)
        s_refs[slot][:, lo:] = s
        mx_refs[slot][:, lo:] = jnp.max(s, axis=0, keepdims=True)

    def accumulate(blk, s, mx, lo=0):
        start = pl.multiple_of(blk * B_TK, B_TK)
        m_old = m_ref[:, lo:]
        m_new = jnp.maximum(m_old, mx)
        alpha = jnp.exp2(m_old - m_new)
        p = jnp.exp2(s - m_new).astype(BF16)
        vt = vt_ref[0, :, pl.ds(start, B_TK)]
        acc_ref[:, lo:] = alpha * acc_ref[:, lo:] + jnp.dot(vt, p, preferred_element_type=F32)
        m_ref[:, lo:] = m_new

    def diagonal_mask(s):
        kc = lax.broadcasted_iota(jnp.int32, s.shape, 0) // CHUNK
        qc = lax.broadcasted_iota(jnp.int32, s.shape, 1) // CHUNK
        return jnp.where(kc <= qc, s, NEG_INF)

    for u in range(B_AHEAD):
        scores(u, u, 0)

    def query_tile(qi, carry):
        q0 = qi * B_TQ
        m_ref[...] = jnp.full(m_ref.shape, NEG_INF, F32)
        acc_ref[...] = jnp.zeros(acc_ref.shape, F32)

        def steps(first_blk, n_blk):
            for u in range(n_blk):
                scores(first_blk + u + B_AHEAD, (u + B_AHEAD) % B_NB, q0)
                accumulate(first_blk + u, s_refs[u % B_NB][...], mx_refs[u % B_NB][...])

        def body(t, c):
            steps(t * 2 * B_NB, 2 * B_NB)
            return c

        lax.fori_loop(0, qi // 2, body, 0)

        @pl.when(qi % 2 == 1)
        def _():
            steps((qi - 1) * B_NB, B_NB)

        q0_next = jnp.minimum(qi + 1, n_q - 1) * B_TQ
        for u in range(B_NB):
            lo = u * B_TK
            if u + B_AHEAD < B_NB:
                scores(qi * B_NB + u + B_AHEAD, u + B_AHEAD, q0, lo=(u + B_AHEAD) * B_TK)
            else:
                scores(u + B_AHEAD - B_NB, u + B_AHEAD - B_NB, q0_next)
            s = diagonal_mask(s_refs[u][:, lo:])
            accumulate(qi * B_NB + u, s, jnp.max(s, axis=0, keepdims=True), lo=lo)
        denom = acc_ref[B_V_DIM:B_V_DIM + 1, :]
        o_ref[0, :, pl.ds(pl.multiple_of(q0, B_TQ), B_TQ)] = (acc_ref[0:B_V_DIM, :] * (1.0 / denom)).astype(BF16)
        return carry

    lax.fori_loop(0, n_q, query_tile, 0)


def _mixer_b(qb, kb, vbt):
    B, S, _ = qb.shape
    return pl.pallas_call(
        _mixer_b_kernel,
        grid=(B, B_HEADS),
        in_specs=[pl.BlockSpec((1, S, LANES), lambda b, h: (b, 0, h)),
                  pl.BlockSpec((1, S, LANES), lambda b, h: (b, 0, h)),
                  pl.BlockSpec((1, VT_ROWS, S), lambda b, h: (b, h, 0))],
        out_specs=pl.BlockSpec((1, B_V_DIM, S), lambda b, h: (b, h, 0)),
        out_shape=jax.ShapeDtypeStruct((B, B_WIDTH, S), BF16),
        scratch_shapes=[pltpu.VMEM((B_TK, B_TQ), F32)] * B_NB + [pltpu.VMEM((1, B_TQ), F32)] * B_NB
                       + [pltpu.VMEM((1, B_TQ), F32), pltpu.VMEM((VT_ROWS, B_TQ), F32)],
        compiler_params=pltpu.CompilerParams(dimension_semantics=("arbitrary", "arbitrary"),
                                             vmem_limit_bytes=VMEM_LIMIT),
        name="mixer_b",
    )(qb, kb, vbt)


def _merge_kernel(x_ref, yat_ref, ybt_ref, lng_ref, lnb_ref, wzg_ref, bzg_ref, wpa_ref, wpb_ref, wout_ref,
                  pg_ref, pb_ref, o_ref):
    rows = x_ref.shape[1] // MERGE_GROUPS
    for g in range(MERGE_GROUPS):
        r = slice(g * rows, (g + 1) * rows)
        h = _layer_norm(x_ref[0, r, :], lng_ref[...], lnb_ref[...])
        zg = jnp.dot(h.astype(BF16), wzg_ref[...], preferred_element_type=F32) + bzg_ref[...]
        za = zg[:, 0:A_WIDTH]
        zb = zg[:, A_WIDTH:A_WIDTH + B_WIDTH]
        ga = zg[:, A_WIDTH + B_WIDTH:A_WIDTH + B_WIDTH + D_MODEL]
        gb = zg[:, A_WIDTH + B_WIDTH + D_MODEL:]
        ya_in = yat_ref[0, :, r].astype(F32).T * (za * jax.nn.sigmoid(za))
        yb_in = ybt_ref[0, :, r].astype(F32).T * (zb * jax.nn.sigmoid(zb))
        ya = jnp.dot(ya_in.astype(BF16), wpa_ref[...], preferred_element_type=F32)
        yb = jnp.dot(yb_in.astype(BF16), wpb_ref[...], preferred_element_type=F32)
        mixed = jax.nn.sigmoid(ga) * ya + jax.nn.sigmoid(gb) * yb
        out = jnp.dot(mixed.astype(BF16), wout_ref[...], preferred_element_type=F32)
        o_ref[0, r, :] = _layer_norm(DEEPNORM_ALPHA * h + out, pg_ref[...], pb_ref[...])


def _merge(x, yat, ybt, lng, lnb, wzg, bzg, wpa, wpb, wout, pg, pb):
    B, S, _ = x.shape
    tm = OUT_TM
    row = lambda w: pl.BlockSpec((1, tm, w), lambda b, i: (b, i, 0))
    nzg = wzg.shape[1]
    return pl.pallas_call(
        _merge_kernel,
        grid=(B, S // tm),
        in_specs=[row(D_MODEL), pl.BlockSpec((1, A_WIDTH, tm), lambda b, i: (b, 0, i)),
                  pl.BlockSpec((1, B_WIDTH, tm), lambda b, i: (b, 0, i)),
                  _const_spec((1, D_MODEL)), _const_spec((1, D_MODEL)),
                  _const_spec((D_MODEL, nzg)), _const_spec((1, nzg)),
                  _const_spec((A_WIDTH, D_MODEL)), _const_spec((B_WIDTH, D_MODEL)),
                  _const_spec((D_MODEL, D_MODEL)), _const_spec((1, D_MODEL)), _const_spec((1, D_MODEL))],
        out_specs=row(D_MODEL),
        out_shape=jax.ShapeDtypeStruct((B, S, D_MODEL), F32),
        compiler_params=pltpu.CompilerParams(dimension_semantics=("arbitrary", "arbitrary"),
                                             vmem_limit_bytes=VMEM_LIMIT),
        name="merge_out",
    )(x, yat, ybt, lng, lnb, wzg, bzg, wpa, wpb, wout, pg, pb)


def _rot_cols(w):
    half = w.shape[-1] // 2
    return jnp.concatenate([-w[..., half:], w[..., :half]], axis=-1)


def _prep_layer(w_in, b_in, w_uq, w_ukv):
    c = 0
    cols = {}
    for name, width in (("aq", A_WIDTH), ("ak", A_WIDTH), ("av", A_WIDTH), ("az", A_WIDTH), ("cq", Q_LORA),
                        ("ckv", KV_LORA), ("kr", B_ROPE_DIM), ("bz", B_WIDTH), ("ga", D_MODEL), ("gb", D_MODEL)):
        cols[name] = slice(c, c + width)
        c += width
    w = lambda n: w_in[:, cols[n]]
    b = lambda n: b_in[cols[n]]
    a_scale = A_HEAD_DIM ** -0.5 * LOG2E
    w1 = jnp.concatenate([w("aq") * a_scale, w("ak"), w("cq"), w("ckv"),
                          w("kr"), w("kr"), _rot_cols(w("kr")), _rot_cols(w("kr"))], axis=1)
    b1 = jnp.concatenate([b("aq") * a_scale, b("ak"), b("cq"), b("ckv"),
                          b("kr"), b("kr"), _rot_cols(b("kr")), _rot_cols(b("kr"))])
    wzg = jnp.concatenate([w("az"), w("bz"), w("ga"), w("gb")], axis=1)
    bzg = jnp.concatenate([b("az"), b("bz"), b("ga"), b("gb")])

    uq = w_uq.reshape(Q_LORA, B_HEADS, B_QK_DIM)
    uq_rope = uq[:, :, B_NOPE_DIM:]
    wq = jnp.concatenate([uq, _rot_cols(uq_rope)], axis=-1).reshape(Q_LORA, B_HEADS * LANES)
    ukv = w_ukv.reshape(KV_LORA, B_HEADS, B_NOPE_DIM + B_V_DIM)
    wk = jnp.concatenate([ukv[:, :, :B_NOPE_DIM], jnp.zeros((KV_LORA, B_HEADS, LANES - B_NOPE_DIM), F32)],
                         axis=-1).reshape(KV_LORA, B_HEADS * LANES)
    wvt = ukv[:, :, B_NOPE_DIM:].reshape(KV_LORA, B_WIDTH).T
    return (w1.astype(BF16), b1[None, :], w("av").T.astype(BF16), b("av")[:, None], wq.astype(BF16),
            wk.astype(BF16), wvt.astype(BF16), wzg.astype(BF16), bzg[None, :])


def _rope_freq_row():
    half = B_ROPE_DIM // 2
    inv_freq = ROPE_THETA ** (-jnp.arange(half, dtype=F32) / half)
    return jnp.tile(inv_freq, LANES // half)[None, :]


def kernel(x, positions, ln_in_g, ln_in_b, w_in, b_in, q_norm_g, kv_norm_g, w_uq, w_ukv, rel_bias, w_proj_a,
           w_proj_b, w_out, ln_post_g, ln_post_b):
    depth = w_in.shape[0]
    assert depth == 1, "the trunk-entry norm is recomputed per kernel, which is only valid for one layer"
    B, S, _ = x.shape
    pos3 = positions.reshape(B, 1, S)
    lng, lnb = ln_in_g[None, :], ln_in_b[None, :]
    freq = _rope_freq_row()
    l = 0
    w1, b1, wavt, bav, wq, wk, wvt, wzg, bzg = _prep_layer(w_in[l], b_in[l], w_uq[l], w_ukv[l])
    aq, ak, avt, qb, kb, vbt = _token_projections(x, pos3, lng, lnb, w1, b1, wavt, bav, q_norm_g[l][None, :],
                                                  kv_norm_g[l][None, :], wq, wk, wvt, freq)
    yat = _mixer_a(aq, ak, avt, _mixer_a_bias(rel_bias[l]))
    ybt = _mixer_b(qb, kb, vbt)
    return _merge(x, yat, ybt, lng, lnb, wzg, bzg, w_proj_a[l].astype(BF16), w_proj_b[l].astype(BF16),
                  w_out[l].astype(BF16), ln_post_g[l][None, :], ln_post_b[l][None, :])
```

```python
import jax
import jax.numpy as jnp
from jax import lax
from jax.experimental import pallas as pl
from jax.experimental.pallas import tpu as pltpu

D_MODEL = 1024
CHUNK = 64
A_HEADS = 8
A_HEAD_DIM = 64
A_WIDTH = A_HEADS * A_HEAD_DIM
A_LEFT_CHUNKS = 8
REL_CLIP = 128
B_HEADS = 8
B_NOPE_DIM = 64
B_ROPE_DIM = 32
B_QK_DIM = B_NOPE_DIM + B_ROPE_DIM
B_V_DIM = 64
B_WIDTH = B_HEADS * B_V_DIM
Q_LORA = 256
KV_LORA = 128
ROPE_THETA = 10000.0
DEEPNORM_ALPHA = 2.0 ** 0.25
LN_EPS = 1e-5
RMS_EPS = 1e-6
NEG_INF = -1e30

LANES = 128
VMEM_LIMIT = 56 * 1024 * 1024

PROJ_TM = 1024
A_TQ = 256
A_WIN = A_TQ + A_LEFT_CHUNKS * CHUNK
A_SUB = 2
A_STEP = A_SUB * A_TQ
A_ROLL = 1024
B_TQ = 1024
B_TK = 256
B_NB = B_TQ // B_TK
B_AHEAD = 2
V_DIM = 64
VT_ROWS = 80
LOG2E = 1.4426950408889634
OUT_TM = 512
MERGE_GROUPS = 2

BF16 = jnp.bfloat16
F32 = jnp.float32


def _layer_norm(x, g, b):
    mu = jnp.mean(x, axis=-1, keepdims=True)
    xc = x - mu
    var = jnp.mean(xc * xc, axis=-1, keepdims=True)
    return xc * lax.rsqrt(var + LN_EPS) * g + b


def _rms_norm(x, g):
    return x * lax.rsqrt(jnp.mean(x * x, axis=-1, keepdims=True) + RMS_EPS) * g


def _const_spec(shape):
    nd = len(shape)
    return pl.BlockSpec(shape, lambda *_: (0,) * nd, pipeline_mode=pl.Buffered(1))


def _store_values_t(ref, vt):
    tail = VT_ROWS - V_DIM
    ones_rows = (lax.broadcasted_iota(jnp.int32, (tail, vt.shape[1]), 0) == 0).astype(BF16)
    for hd in range(vt.shape[0] // V_DIM):
        ref[0, hd * VT_ROWS:hd * VT_ROWS + V_DIM, :] = vt[hd * V_DIM:(hd + 1) * V_DIM].astype(BF16)
        ref[0, hd * VT_ROWS + V_DIM:(hd + 1) * VT_ROWS, :] = ones_rows


def _proj_kernel(x_ref, pos_ref, lng_ref, lnb_ref, w1_ref, b1_ref, wavt_ref, bav_ref, qg_ref, kvg_ref, wqt_ref,
                 wk_ref, wvt_ref, freq_ref, aq_ref, ak_ref, avt_ref, qbt_ref, kb_ref, vbt_ref):
    hb = _layer_norm(x_ref[0], lng_ref[...], lnb_ref[...]).astype(BF16)
    proj = jnp.dot(hb, w1_ref[...], preferred_element_type=F32) + b1_ref[...]
    aq_ref[0] = proj[:, 0:A_WIDTH].astype(BF16)
    ak_ref[0] = proj[:, A_WIDTH:2 * A_WIDTH].astype(BF16)
    avt = lax.dot_general(wavt_ref[...], hb, (((1,), (1,)), ((), ())), preferred_element_type=F32) + bav_ref[...]
    _store_values_t(avt_ref, avt)
    o = 2 * A_WIDTH
    cq = _rms_norm(proj[:, o:o + Q_LORA], qg_ref[...]).astype(BF16)
    o += Q_LORA
    ckv = _rms_norm(proj[:, o:o + KV_LORA], kvg_ref[...]).astype(BF16)
    o += KV_LORA
    kr = proj[:, o:o + LANES]

    pos_rows = jnp.broadcast_to(pos_ref[0].astype(F32), (LANES, pos_ref.shape[2])).T
    ang = pos_rows * freq_ref[...]
    cos_t = jnp.cos(ang)
    sin_t = jnp.sin(ang)
    lane = lax.broadcasted_iota(jnp.int32, (1, LANES), 1)
    scale = B_QK_DIM ** -0.5 * LOG2E
    tq = jnp.broadcast_to(jnp.where(lane < B_NOPE_DIM, scale, jnp.where(lane < B_QK_DIM, cos_t, sin_t) * scale),
                          cos_t.shape)
    prod = kr * jnp.where(lane < B_NOPE_DIM, cos_t, sin_t)
    k_rope = jnp.where(lane >= B_NOPE_DIM, prod + pltpu.roll(prod, LANES // 2, 1), 0.0)

    q_all_t = lax.dot_general(wqt_ref[...], cq, (((1,), (1,)), ((), ())), preferred_element_type=F32)
    tq_t = tq.T
    k_all = jnp.dot(ckv, wk_ref[...], preferred_element_type=F32)
    for hd in range(B_HEADS):
        sl = slice(hd * LANES, (hd + 1) * LANES)
        qbt_ref[0, sl, :] = (q_all_t[sl, :] * tq_t).astype(BF16)
        kb_ref[0, :, sl] = (k_all[:, sl] + k_rope).astype(BF16)
    vt = lax.dot_general(wvt_ref[...], ckv, (((1,), (1,)), ((), ())), preferred_element_type=F32)
    _store_values_t(vbt_ref, vt)


def _token_projections(x, pos3, lng, lnb, w1, b1, wavt, bav, qg, kvg, wqt, wk, wvt, freq):
    B, S, _ = x.shape
    tm = PROJ_TM
    n1 = w1.shape[1]
    row = lambda w: pl.BlockSpec((1, tm, w), lambda b, i: (b, i, 0))
    vt_spec = pl.BlockSpec((1, B_HEADS * VT_ROWS, tm), lambda b, i: (b, 0, i))
    vt_shape = jax.ShapeDtypeStruct((B, B_HEADS * VT_ROWS, S), BF16)
    return pl.pallas_call(
        _proj_kernel,
        grid=(B, S // tm),
        in_specs=[row(D_MODEL), pl.BlockSpec((1, 1, tm), lambda b, i: (b, 0, i)),
                  _const_spec((1, D_MODEL)), _const_spec((1, D_MODEL)),
                  _const_spec((D_MODEL, n1)), _const_spec((1, n1)),
                  _const_spec((A_WIDTH, D_MODEL)), _const_spec((A_WIDTH, 1)),
                  _const_spec((1, Q_LORA)), _const_spec((1, KV_LORA)),
                  _const_spec((B_HEADS * LANES, Q_LORA)), _const_spec((KV_LORA, B_HEADS * LANES)),
                  _const_spec((B_WIDTH, KV_LORA)), _const_spec((1, LANES))],
        out_specs=[row(A_WIDTH), row(A_WIDTH), vt_spec,
                   pl.BlockSpec((1, B_HEADS * LANES, tm), lambda b, i: (b, 0, i)), row(B_HEADS * LANES), vt_spec],
        out_shape=[jax.ShapeDtypeStruct((B, S, A_WIDTH), BF16)] * 2 + [vt_shape]
                  + [jax.ShapeDtypeStruct((B, B_HEADS * LANES, S), BF16),
                     jax.ShapeDtypeStruct((B, S, B_HEADS * LANES), BF16), vt_shape],
        compiler_params=pltpu.CompilerParams(dimension_semantics=("arbitrary", "arbitrary"),
                                             vmem_limit_bytes=VMEM_LIMIT),
        name="token_projections",
    )(x, pos3, lng, lnb, w1, b1, wavt, bav, qg, kvg, wqt, wk, wvt, freq)


def _mixer_a_kernel(q_ref, k1_ref, k0_ref, v1_ref, v0_ref, bias_ref, o_ref, s0_ref, s1_ref, mx0_ref, mx1_ref):
    i = pl.program_id(1)
    s_refs, mx_refs = (s0_ref, s1_ref), (mx0_ref, mx1_ref)
    n_old = A_LEFT_CHUNKS * CHUNK
    lane = lax.broadcasted_iota(jnp.int32, (1, LANES), 1)
    row = lax.broadcasted_iota(jnp.int32, (n_old, 1), 0)

    def scores(unit, slot):
        t, pair = divmod(unit, A_HEADS // 2)
        sl = slice(pair * LANES, (pair + 1) * LANES)
        qp = q_ref[0, t * A_TQ:(t + 1) * A_TQ, sl]
        kwin = jnp.concatenate([k1_ref[0, :, sl], k0_ref[0, :, sl]], axis=0)
        kp = kwin[t * A_TQ:t * A_TQ + A_WIN]
        q2 = jnp.concatenate([jnp.where((lane // A_HEAD_DIM) == hh, qp, jnp.zeros_like(qp)) for hh in range(2)],
                             axis=0)
        s = lax.dot_general(kp, q2, (((1,), (1,)), ((), ())), preferred_element_type=F32)
        s = s + jnp.concatenate([bias_ref[2 * pair], bias_ref[2 * pair + 1]], axis=1)
        kpos = row + (i * A_STEP + t * A_TQ - n_old)
        pad_mask = jnp.where(kpos >= 0, 0.0, NEG_INF).astype(F32)
        s = jnp.concatenate([s[:n_old] + pad_mask, s[n_old:]], axis=0)
        s_refs[slot][...] = s
        mx_refs[slot][...] = jnp.max(s, axis=0, keepdims=True)

    def finish(unit, slot):
        t, pair = divmod(unit, A_HEADS // 2)
        p = jnp.exp2(s_refs[slot][...] - mx_refs[slot][...]).astype(BF16)
        rows = slice(2 * pair * VT_ROWS, (2 * pair + 2) * VT_ROWS)
        vwin = jnp.concatenate([v1_ref[0, rows, :], v0_ref[0, rows, :]], axis=1)
        vt = vwin[:, t * A_TQ:t * A_TQ + A_WIN]
        for hh in range(2):
            oh = jnp.dot(vt[hh * VT_ROWS:(hh + 1) * VT_ROWS], p[:, hh * A_TQ:(hh + 1) * A_TQ],
                         preferred_element_type=F32)
            hd = 2 * pair + hh
            o_ref[0, hd * V_DIM:(hd + 1) * V_DIM, t * A_TQ:(t + 1) * A_TQ] = (
                oh[:V_DIM] * (1.0 / oh[V_DIM:V_DIM + 1])).astype(BF16)

    n_units = A_SUB * A_HEADS // 2
    scores(0, 0)
    for unit in range(n_units):
        if unit + 1 < n_units:
            scores(unit + 1, (unit + 1) % 2)
        finish(unit, unit % 2)


def _mixer_a(aq, ak, avt, bias):
    B, S, W = aq.shape
    blk = lambda back: pl.BlockSpec((1, A_STEP, W), lambda b, i: (b, jnp.maximum(i - back, 0), 0))
    vblk = lambda back: pl.BlockSpec((1, A_HEADS * VT_ROWS, A_STEP), lambda b, i: (b, 0, jnp.maximum(i - back, 0)))
    return pl.pallas_call(
        _mixer_a_kernel,
        grid=(B, S // A_STEP),
        in_specs=[blk(0), blk(1), blk(0), vblk(1), vblk(0), _const_spec(bias.shape)],
        out_specs=pl.BlockSpec((1, W, A_STEP), lambda b, i: (b, 0, i)),
        out_shape=jax.ShapeDtypeStruct((B, W, S), BF16),
        scratch_shapes=[pltpu.VMEM((A_WIN, 2 * A_TQ), F32)] * 2 + [pltpu.VMEM((1, 2 * A_TQ), F32)] * 2,
        compiler_params=pltpu.CompilerParams(dimension_semantics=("arbitrary", "arbitrary"),
                                             vmem_limit_bytes=VMEM_LIMIT),
        name="mixer_a",
    )(aq, ak, ak, avt, avt, bias)


def _bias_kernel(row_ref, o_ref):
    rows = jnp.broadcast_to(row_ref[0], (A_WIN, A_ROLL))
    table = pltpu.roll(rows, 0, 1, stride=1, stride_axis=0)[:, :A_TQ]
    kc = lax.broadcasted_iota(jnp.int32, (A_WIN, A_TQ), 0) // CHUNK
    qc = lax.broadcasted_iota(jnp.int32, (A_WIN, A_TQ), 1) // CHUNK
    gap = qc + A_LEFT_CHUNKS - kc
    o_ref[0] = jnp.where((gap >= 0) & (gap <= A_LEFT_CHUNKS), table * LOG2E, NEG_INF)


def _mixer_a_bias(rel_bias):
    tbl = rel_bias.T.astype(F32)
    n_old = A_LEFT_CHUNKS * CHUNK
    first_tbl = A_ROLL - n_old - REL_CLIP
    far = jnp.broadcast_to(tbl[:, -1:], (A_HEADS, A_ROLL))
    near = jnp.broadcast_to(tbl[:, :1], (A_HEADS, A_ROLL))
    row = jnp.concatenate([far[:, :A_ROLL - A_WIN + 1], near[:, A_ROLL - A_WIN + 1:first_tbl], tbl,
                           far[:, first_tbl + tbl.shape[1]:]], axis=1)
    row = row[:, None, :]
    return pl.pallas_call(
        _bias_kernel,
        grid=(A_HEADS,),
        in_specs=[pl.BlockSpec((1, 1, A_ROLL), lambda h: (h, 0, 0))],
        out_specs=pl.BlockSpec((1, A_WIN, A_TQ), lambda h: (h, 0, 0)),
        out_shape=jax.ShapeDtypeStruct((A_HEADS, A_WIN, A_TQ), F32),
        name="mixer_a_bias",
    )(row)


def _mixer_b_kernel(q_ref, k_ref, vt_ref, o_ref, *scratch):
    s_refs = scratch[0:B_NB]
    mx_refs = scratch[B_NB:2 * B_NB]
    m_ref, acc_ref = scratch[2 * B_NB:]
    n_q = q_ref.shape[2] // B_TQ

    def scores(blk, slot, q0, lo=0):
        start = pl.multiple_of(blk * B_TK, B_TK)
        k = k_ref[0, pl.ds(start, B_TK), :]
        q = q_ref[0, :, pl.ds(pl.multiple_of(q0 + lo, B_TK), B_TQ - lo)]
        s = jnp.dot(k, q, preferred_element_type=F32)
        s_refs[slot][:, lo:] = s
        mx_refs[slot][:, lo:] = jnp.max(s, axis=0, keepdims=True)

    def accumulate(blk, s, mx, lo=0):
        start = pl.multiple_of(blk * B_TK, B_TK)
        m_old = m_ref[:, lo:]
        m_new = jnp.maximum(m_old, mx)
        alpha = jnp.exp2(m_old - m_new)
        p = jnp.exp2(s - m_new).astype(BF16)
        vt = vt_ref[0, :, pl.ds(start, B_TK)]
        acc_ref[:, lo:] = alpha * acc_ref[:, lo:] + jnp.dot(vt, p, preferred_element_type=F32)
        m_ref[:, lo:] = m_new

    def diagonal_mask(s):
        kc = lax.broadcasted_iota(jnp.int32, s.shape, 0) // CHUNK
        qc = lax.broadcasted_iota(jnp.int32, s.shape, 1) // CHUNK
        return jnp.where(kc <= qc, s, NEG_INF)

    for u in range(B_AHEAD):
        scores(u, u, 0)

    def query_tile(qi, carry):
        q0 = qi * B_TQ
        m_ref[...] = jnp.full(m_ref.shape, NEG_INF, F32)
        acc_ref[...] = jnp.zeros(acc_ref.shape, F32)

        def steps(first_blk, n_blk):
            for u in range(n_blk):
                scores(first_blk + u + B_AHEAD, (u + B_AHEAD) % B_NB, q0)
                accumulate(first_blk + u, s_refs[u % B_NB][...], mx_refs[u % B_NB][...])

        def body(t, c):
            steps(t * 2 * B_NB, 2 * B_NB)
            return c

        lax.fori_loop(0, qi // 2, body, 0)

        @pl.when(qi % 2 == 1)
        def _():
            steps((qi - 1) * B_NB, B_NB)

        q0_next = jnp.minimum(qi + 1, n_q - 1) * B_TQ
        for u in range(B_NB):
            lo = u * B_TK
            if u + B_AHEAD < B_NB:
                scores(qi * B_NB + u + B_AHEAD, u + B_AHEAD, q0, lo=(u + B_AHEAD) * B_TK)
            else:
                scores(u + B_AHEAD - B_NB, u + B_AHEAD - B_NB, q0_next)
            s = diagonal_mask(s_refs[u][:, lo:])
            accumulate(qi * B_NB + u, s, jnp.max(s, axis=0, keepdims=True), lo=lo)
        denom = acc_ref[B_V_DIM:B_V_DIM + 1, :]
        o_ref[0, :, pl.ds(pl.multiple_of(q0, B_TQ), B_TQ)] = (acc_ref[0:B_V_DIM, :] * (1.0 / denom)).astype(BF16)
        return carry

    lax.fori_loop(0, n_q, query_tile, 0)


def _mixer_b(qbt, kb, vbt):
    B, S, _ = kb.shape
    return pl.pallas_call(
        _mixer_b_kernel,
        grid=(B, B_HEADS),
        in_specs=[pl.BlockSpec((1, LANES, S), lambda b, h: (b, h, 0)),
                  pl.BlockSpec((1, S, LANES), lambda b, h: (b, 0, h)),
                  pl.BlockSpec((1, VT_ROWS, S), lambda b, h: (b, h, 0))],
        out_specs=pl.BlockSpec((1, B_V_DIM, S), lambda b, h: (b, h, 0)),
        out_shape=jax.ShapeDtypeStruct((B, B_WIDTH, S), BF16),
        scratch_shapes=[pltpu.VMEM((B_TK, B_TQ), F32)] * B_NB + [pltpu.VMEM((1, B_TQ), F32)] * B_NB
                       + [pltpu.VMEM((1, B_TQ), F32), pltpu.VMEM((VT_ROWS, B_TQ), F32)],
        compiler_params=pltpu.CompilerParams(dimension_semantics=("arbitrary", "arbitrary"),
                                             vmem_limit_bytes=VMEM_LIMIT),
        name="mixer_b",
    )(qbt, kb, vbt)


def _merge_kernel(x_ref, yat_ref, ybt_ref, lng_ref, lnb_ref, wzg_ref, bzg_ref, wpa_ref, wpb_ref, wout_ref,
                  pg_ref, pb_ref, o_ref):
    rows = x_ref.shape[1] // MERGE_GROUPS
    for g in range(MERGE_GROUPS):
        r = slice(g * rows, (g + 1) * rows)
        h = _layer_norm(x_ref[0, r, :], lng_ref[...], lnb_ref[...])
        zg = jnp.dot(h.astype(BF16), wzg_ref[...], preferred_element_type=F32) + bzg_ref[...]
        za = zg[:, 0:A_WIDTH]
        zb = zg[:, A_WIDTH:A_WIDTH + B_WIDTH]
        ga = zg[:, A_WIDTH + B_WIDTH:A_WIDTH + B_WIDTH + D_MODEL]
        gb = zg[:, A_WIDTH + B_WIDTH + D_MODEL:]
        ya_in = yat_ref[0, :, r].astype(F32).T * (za * jax.nn.sigmoid(za))
        yb_in = ybt_ref[0, :, r].astype(F32).T * (zb * jax.nn.sigmoid(zb))
        ya = jnp.dot(ya_in.astype(BF16), wpa_ref[...], preferred_element_type=F32)
        yb = jnp.dot(yb_in.astype(BF16), wpb_ref[...], preferred_element_type=F32)
        mixed = jax.nn.sigmoid(ga) * ya + jax.nn.sigmoid(gb) * yb
        out = jnp.dot(mixed.astype(BF16), wout_ref[...], preferred_element_type=F32)
        o_ref[0, r, :] = _layer_norm(DEEPNORM_ALPHA * h + out, pg_ref[...], pb_ref[...])


def _merge(x, yat, ybt, lng, lnb, wzg, bzg, wpa, wpb, wout, pg, pb):
    B, S, _ = x.shape
    tm = OUT_TM
    row = lambda w: pl.BlockSpec((1, tm, w), lambda b, i: (b, i, 0))
    nzg = wzg.shape[1]
    return pl.pallas_call(
        _merge_kernel,
        grid=(B, S // tm),
        in_specs=[row(D_MODEL), pl.BlockSpec((1, A_WIDTH, tm), lambda b, i: (b, 0, i)),
                  pl.BlockSpec((1, B_WIDTH, tm), lambda b, i: (b, 0, i)),
                  _const_spec((1, D_MODEL)), _const_spec((1, D_MODEL)),
                  _const_spec((D_MODEL, nzg)), _const_spec((1, nzg)),
                  _const_spec((A_WIDTH, D_MODEL)), _const_spec((B_WIDTH, D_MODEL)),
                  _const_spec((D_MODEL, D_MODEL)), _const_spec((1, D_MODEL)), _const_spec((1, D_MODEL))],
        out_specs=row(D_MODEL),
        out_shape=jax.ShapeDtypeStruct((B, S, D_MODEL), F32),
        compiler_params=pltpu.CompilerParams(dimension_semantics=("arbitrary", "arbitrary"),
                                             vmem_limit_bytes=VMEM_LIMIT),
        name="merge_out",
    )(x, yat, ybt, lng, lnb, wzg, bzg, wpa, wpb, wout, pg, pb)


def _rot_cols(w):
    half = w.shape[-1] // 2
    return jnp.concatenate([-w[..., half:], w[..., :half]], axis=-1)


def _prep_layer(w_in, b_in, w_uq, w_ukv):
    c = 0
    cols = {}
    for name, width in (("aq", A_WIDTH), ("ak", A_WIDTH), ("av", A_WIDTH), ("az", A_WIDTH), ("cq", Q_LORA),
                        ("ckv", KV_LORA), ("kr", B_ROPE_DIM), ("bz", B_WIDTH), ("ga", D_MODEL), ("gb", D_MODEL)):
        cols[name] = slice(c, c + width)
        c += width
    w = lambda n: w_in[:, cols[n]]
    b = lambda n: b_in[cols[n]]
    a_scale = A_HEAD_DIM ** -0.5 * LOG2E
    w1 = jnp.concatenate([w("aq") * a_scale, w("ak"), w("cq"), w("ckv"),
                          w("kr"), w("kr"), _rot_cols(w("kr")), _rot_cols(w("kr"))], axis=1)
    b1 = jnp.concatenate([b("aq") * a_scale, b("ak"), b("cq"), b("ckv"),
                          b("kr"), b("kr"), _rot_cols(b("kr")), _rot_cols(b("kr"))])
    wzg = jnp.concatenate([w("az"), w("bz"), w("ga"), w("gb")], axis=1)
    bzg = jnp.concatenate([b("az"), b("bz"), b("ga"), b("gb")])

    uq = w_uq.reshape(Q_LORA, B_HEADS, B_QK_DIM)
    uq_rope = uq[:, :, B_NOPE_DIM:]
    wq = jnp.concatenate([uq, _rot_cols(uq_rope)], axis=-1).reshape(Q_LORA, B_HEADS * LANES)
    ukv = w_ukv.reshape(KV_LORA, B_HEADS, B_NOPE_DIM + B_V_DIM)
    wk = jnp.concatenate([ukv[:, :, :B_NOPE_DIM], jnp.zeros((KV_LORA, B_HEADS, LANES - B_NOPE_DIM), F32)],
                         axis=-1).reshape(KV_LORA, B_HEADS * LANES)
    wvt = ukv[:, :, B_NOPE_DIM:].reshape(KV_LORA, B_WIDTH).T
    return (w1.astype(BF16), b1[None, :], w("av").T.astype(BF16), b("av")[:, None], wq.T.astype(BF16),
            wk.astype(BF16), wvt.astype(BF16), wzg.astype(BF16), bzg[None, :])


def _rope_freq_row():
    half = B_ROPE_DIM // 2
    inv_freq = ROPE_THETA ** (-jnp.arange(half, dtype=F32) / half)
    return jnp.tile(inv_freq, LANES // half)[None, :]


def kernel(x, positions, ln_in_g, ln_in_b, w_in, b_in, q_norm_g, kv_norm_g, w_uq, w_ukv, rel_bias, w_proj_a,
           w_proj_b, w_out, ln_post_g, ln_post_b):
    depth = w_in.shape[0]
    assert depth == 1, "the trunk-entry norm is recomputed per kernel, which is only valid for one layer"
    B, S, _ = x.shape
    pos3 = positions.reshape(B, 1, S)
    lng, lnb = ln_in_g[None, :], ln_in_b[None, :]
    freq = _rope_freq_row()
    l = 0
    w1, b1, wavt, bav, wqt, wk, wvt, wzg, bzg = _prep_layer(w_in[l], b_in[l], w_uq[l], w_ukv[l])
    aq, ak, avt, qbt, kb, vbt = _token_projections(x, pos3, lng, lnb, w1, b1, wavt, bav, q_norm_g[l][None, :],
                                                   kv_norm_g[l][None, :], wqt, wk, wvt, freq)
    yat = _mixer_a(aq, ak, avt, _mixer_a_bias(rel_bias[l]))
    ybt = _mixer_b(qbt, kb, vbt)
    return _merge(x, yat, ybt, lng, lnb, wzg, bzg, w_proj_a[l].astype(BF16), w_proj_b[l].astype(BF16),
                  w_out[l].astype(BF16), ln_post_g[l][None, :], ln_post_b[l][None, :])
```

```python
import jax
import jax.numpy as jnp
from jax import lax
from jax.experimental import pallas as pl
from jax.experimental.pallas import tpu as pltpu

D_MODEL = 1024
CHUNK = 64
A_HEADS = 8
A_HEAD_DIM = 64
A_WIDTH = A_HEADS * A_HEAD_DIM
A_LEFT_CHUNKS = 8
REL_CLIP = 128
B_HEADS = 8
B_NOPE_DIM = 64
B_ROPE_DIM = 32
B_QK_DIM = B_NOPE_DIM + B_ROPE_DIM
B_V_DIM = 64
B_WIDTH = B_HEADS * B_V_DIM
Q_LORA = 256
KV_LORA = 128
ROPE_THETA = 10000.0
DEEPNORM_ALPHA = 2.0 ** 0.25
LN_EPS = 1e-5
RMS_EPS = 1e-6
NEG_INF = -1e30

LANES = 128
VMEM_LIMIT = 56 * 1024 * 1024

PROJ_TM = 1024
A_TQ = 256
A_WIN = A_TQ + A_LEFT_CHUNKS * CHUNK
A_SUB = 2
A_STEP = A_SUB * A_TQ
A_ROLL = 1024
B_TQ = 1024
B_TK = 256
B_NB = B_TQ // B_TK
B_AHEAD = 2
V_DIM = 64
VT_ROWS = 80
LOG2E = 1.4426950408889634
OUT_TM = 512
MERGE_GROUPS = 2

BF16 = jnp.bfloat16
F32 = jnp.float32


def _layer_norm(x, g, b):
    mu = jnp.mean(x, axis=-1, keepdims=True)
    xc = x - mu
    var = jnp.mean(xc * xc, axis=-1, keepdims=True)
    return xc * lax.rsqrt(var + LN_EPS) * g + b


def _rms_norm(x, g):
    return x * lax.rsqrt(jnp.mean(x * x, axis=-1, keepdims=True) + RMS_EPS) * g


def _const_spec(shape):
    nd = len(shape)
    return pl.BlockSpec(shape, lambda *_: (0,) * nd, pipeline_mode=pl.Buffered(1))


def _store_values_t(ref, vt):
    tail = VT_ROWS - V_DIM
    ones_rows = (lax.broadcasted_iota(jnp.int32, (tail, vt.shape[1]), 0) == 0).astype(BF16)
    for hd in range(vt.shape[0] // V_DIM):
        ref[0, hd * VT_ROWS:hd * VT_ROWS + V_DIM, :] = vt[hd * V_DIM:(hd + 1) * V_DIM].astype(BF16)
        ref[0, hd * VT_ROWS + V_DIM:(hd + 1) * VT_ROWS, :] = ones_rows


def _proj_kernel(x_ref, pos_ref, lng_ref, lnb_ref, w1_ref, b1_ref, wavt_ref, bav_ref, qg_ref, kvg_ref, wqt_ref,
                 wk_ref, wvt_ref, freq_ref, aq_ref, ak_ref, avt_ref, qbt_ref, kb_ref, vbt_ref):
    hb = _layer_norm(x_ref[0], lng_ref[...], lnb_ref[...]).astype(BF16)
    proj = jnp.dot(hb, w1_ref[...], preferred_element_type=F32) + b1_ref[...]
    aq_ref[0] = proj[:, 0:A_WIDTH].astype(BF16)
    ak_ref[0] = proj[:, A_WIDTH:2 * A_WIDTH].astype(BF16)
    avt = lax.dot_general(wavt_ref[...], hb, (((1,), (1,)), ((), ())), preferred_element_type=F32) + bav_ref[...]
    _store_values_t(avt_ref, avt)
    o = 2 * A_WIDTH
    cq = _rms_norm(proj[:, o:o + Q_LORA], qg_ref[...]).astype(BF16)
    o += Q_LORA
    ckv = _rms_norm(proj[:, o:o + KV_LORA], kvg_ref[...]).astype(BF16)
    o += KV_LORA
    kr = proj[:, o:o + LANES]

    ang_t = freq_ref[...] * pos_ref[0].astype(F32)
    cos16 = jnp.cos(ang_t)
    sin16 = jnp.sin(ang_t)
    scale = B_QK_DIM ** -0.5 * LOG2E
    tq_t = jnp.concatenate([jnp.full((B_NOPE_DIM, ang_t.shape[1]), scale, F32),
                            cos16 * scale, cos16 * scale, sin16 * scale, sin16 * scale], axis=0)
    lane = lax.broadcasted_iota(jnp.int32, (1, LANES), 1)
    tk = jnp.concatenate([cos16] * 4 + [sin16] * 4, axis=0).T
    prod = kr * tk
    k_rope = jnp.where(lane >= B_NOPE_DIM, prod + pltpu.roll(prod, LANES // 2, 1), 0.0)

    q_all_t = lax.dot_general(wqt_ref[...], cq, (((1,), (1,)), ((), ())), preferred_element_type=F32)
    k_all = jnp.dot(ckv, wk_ref[...], preferred_element_type=F32)
    for hd in range(B_HEADS):
        sl = slice(hd * LANES, (hd + 1) * LANES)
        qbt_ref[0, sl, :] = (q_all_t[sl, :] * tq_t).astype(BF16)
        kb_ref[0, :, sl] = (k_all[:, sl] + k_rope).astype(BF16)
    vt = lax.dot_general(wvt_ref[...], ckv, (((1,), (1,)), ((), ())), preferred_element_type=F32)
    _store_values_t(vbt_ref, vt)


def _token_projections(x, pos3, lng, lnb, w1, b1, wavt, bav, qg, kvg, wqt, wk, wvt, freq):
    B, S, _ = x.shape
    tm = PROJ_TM
    n1 = w1.shape[1]
    row = lambda w: pl.BlockSpec((1, tm, w), lambda b, i: (b, i, 0))
    vt_spec = pl.BlockSpec((1, B_HEADS * VT_ROWS, tm), lambda b, i: (b, 0, i))
    vt_shape = jax.ShapeDtypeStruct((B, B_HEADS * VT_ROWS, S), BF16)
    return pl.pallas_call(
        _proj_kernel,
        grid=(B, S // tm),
        in_specs=[row(D_MODEL), pl.BlockSpec((1, 1, tm), lambda b, i: (b, 0, i)),
                  _const_spec((1, D_MODEL)), _const_spec((1, D_MODEL)),
                  _const_spec((D_MODEL, n1)), _const_spec((1, n1)),
                  _const_spec((A_WIDTH, D_MODEL)), _const_spec((A_WIDTH, 1)),
                  _const_spec((1, Q_LORA)), _const_spec((1, KV_LORA)),
                  _const_spec((B_HEADS * LANES, Q_LORA)), _const_spec((KV_LORA, B_HEADS * LANES)),
                  _const_spec((B_WIDTH, KV_LORA)), _const_spec((B_ROPE_DIM // 2, 1))],
        out_specs=[row(A_WIDTH), row(A_WIDTH), vt_spec,
                   pl.BlockSpec((1, B_HEADS * LANES, tm), lambda b, i: (b, 0, i)), row(B_HEADS * LANES), vt_spec],
        out_shape=[jax.ShapeDtypeStruct((B, S, A_WIDTH), BF16)] * 2 + [vt_shape]
                  + [jax.ShapeDtypeStruct((B, B_HEADS * LANES, S), BF16),
                     jax.ShapeDtypeStruct((B, S, B_HEADS * LANES), BF16), vt_shape],
        compiler_params=pltpu.CompilerParams(dimension_semantics=("arbitrary", "arbitrary"),
                                             vmem_limit_bytes=VMEM_LIMIT),
        name="token_projections",
    )(x, pos3, lng, lnb, w1, b1, wavt, bav, qg, kvg, wqt, wk, wvt, freq)


def _mixer_a_kernel(q_ref, k1_ref, k0_ref, v1_ref, v0_ref, bias_ref, o_ref, s0_ref, s1_ref, mx0_ref, mx1_ref):
    i = pl.program_id(1)
    s_refs, mx_refs = (s0_ref, s1_ref), (mx0_ref, mx1_ref)
    n_old = A_LEFT_CHUNKS * CHUNK
    lane = lax.broadcasted_iota(jnp.int32, (1, LANES), 1)
    row = lax.broadcasted_iota(jnp.int32, (n_old, 1), 0)

    def scores(unit, slot):
        t, pair = divmod(unit, A_HEADS // 2)
        sl = slice(pair * LANES, (pair + 1) * LANES)
        qp = q_ref[0, t * A_TQ:(t + 1) * A_TQ, sl]
        kwin = jnp.concatenate([k1_ref[0, :, sl], k0_ref[0, :, sl]], axis=0)
        kp = kwin[t * A_TQ:t * A_TQ + A_WIN]
        q2 = jnp.concatenate([jnp.where((lane // A_HEAD_DIM) == hh, qp, jnp.zeros_like(qp)) for hh in range(2)],
                             axis=0)
        s = lax.dot_general(kp, q2, (((1,), (1,)), ((), ())), preferred_element_type=F32)
        s = s + jnp.concatenate([bias_ref[2 * pair], bias_ref[2 * pair + 1]], axis=1)
        kpos = row + (i * A_STEP + t * A_TQ - n_old)
        pad_mask = jnp.where(kpos >= 0, 0.0, NEG_INF).astype(F32)
        s = jnp.concatenate([s[:n_old] + pad_mask, s[n_old:]], axis=0)
        s_refs[slot][...] = s
        mx_refs[slot][...] = jnp.max(s, axis=0, keepdims=True)

    def finish(unit, slot):
        t, pair = divmod(unit, A_HEADS // 2)
        p = jnp.exp2(s_refs[slot][...] - mx_refs[slot][...]).astype(BF16)
        rows = slice(2 * pair * VT_ROWS, (2 * pair + 2) * VT_ROWS)
        vwin = jnp.concatenate([v1_ref[0, rows, :], v0_ref[0, rows, :]], axis=1)
        vt = vwin[:, t * A_TQ:t * A_TQ + A_WIN]
        for hh in range(2):
            oh = jnp.dot(vt[hh * VT_ROWS:(hh + 1) * VT_ROWS], p[:, hh * A_TQ:(hh + 1) * A_TQ],
                         preferred_element_type=F32)
            hd = 2 * pair + hh
            o_ref[0, hd * V_DIM:(hd + 1) * V_DIM, t * A_TQ:(t + 1) * A_TQ] = (
                oh[:V_DIM] * (1.0 / oh[V_DIM:V_DIM + 1])).astype(BF16)

    n_units = A_SUB * A_HEADS // 2
    scores(0, 0)
    for unit in range(n_units):
        if unit + 1 < n_units:
            scores(unit + 1, (unit + 1) % 2)
        finish(unit, unit % 2)


def _mixer_a(aq, ak, avt, bias):
    B, S, W = aq.shape
    blk = lambda back: pl.BlockSpec((1, A_STEP, W), lambda b, i: (b, jnp.maximum(i - back, 0), 0))
    vblk = lambda back: pl.BlockSpec((1, A_HEADS * VT_ROWS, A_STEP), lambda b, i: (b, 0, jnp.maximum(i - back, 0)))
    return pl.pallas_call(
        _mixer_a_kernel,
        grid=(B, S // A_STEP),
        in_specs=[blk(0), blk(1), blk(0), vblk(1), vblk(0), _const_spec(bias.shape)],
        out_specs=pl.BlockSpec((1, W, A_STEP), lambda b, i: (b, 0, i)),
        out_shape=jax.ShapeDtypeStruct((B, W, S), BF16),
        scratch_shapes=[pltpu.VMEM((A_WIN, 2 * A_TQ), F32)] * 2 + [pltpu.VMEM((1, 2 * A_TQ), F32)] * 2,
        compiler_params=pltpu.CompilerParams(dimension_semantics=("arbitrary", "arbitrary"),
                                             vmem_limit_bytes=VMEM_LIMIT),
        name="mixer_a",
    )(aq, ak, ak, avt, avt, bias)


def _bias_kernel(row_ref, o_ref):
    rows = jnp.broadcast_to(row_ref[0], (A_WIN, A_ROLL))
    table = pltpu.roll(rows, 0, 1, stride=1, stride_axis=0)[:, :A_TQ]
    kc = lax.broadcasted_iota(jnp.int32, (A_WIN, A_TQ), 0) // CHUNK
    qc = lax.broadcasted_iota(jnp.int32, (A_WIN, A_TQ), 1) // CHUNK
    gap = qc + A_LEFT_CHUNKS - kc
    o_ref[0] = jnp.where((gap >= 0) & (gap <= A_LEFT_CHUNKS), table * LOG2E, NEG_INF)


def _mixer_a_bias(rel_bias):
    tbl = rel_bias.T.astype(F32)
    n_old = A_LEFT_CHUNKS * CHUNK
    first_tbl = A_ROLL - n_old - REL_CLIP
    far = jnp.broadcast_to(tbl[:, -1:], (A_HEADS, A_ROLL))
    near = jnp.broadcast_to(tbl[:, :1], (A_HEADS, A_ROLL))
    row = jnp.concatenate([far[:, :A_ROLL - A_WIN + 1], near[:, A_ROLL - A_WIN + 1:first_tbl], tbl,
                           far[:, first_tbl + tbl.shape[1]:]], axis=1)
    row = row[:, None, :]
    return pl.pallas_call(
        _bias_kernel,
        grid=(A_HEADS,),
        in_specs=[pl.BlockSpec((1, 1, A_ROLL), lambda h: (h, 0, 0))],
        out_specs=pl.BlockSpec((1, A_WIN, A_TQ), lambda h: (h, 0, 0)),
        out_shape=jax.ShapeDtypeStruct((A_HEADS, A_WIN, A_TQ), F32),
        name="mixer_a_bias",
    )(row)


def _mixer_b_kernel(q_ref, k_ref, vt_ref, o_ref, *scratch):
    s_refs = scratch[0:B_NB]
    mx_refs = scratch[B_NB:2 * B_NB]
    m_ref, acc_ref = scratch[2 * B_NB:]
    n_q = q_ref.shape[2] // B_TQ

    def scores(blk, slot, q0, lo=0):
        start = pl.multiple_of(blk * B_TK, B_TK)
        k = k_ref[0, pl.ds(start, B_TK), :]
        q = q_ref[0, :, pl.ds(pl.multiple_of(q0 + lo, B_TK), B_TQ - lo)]
        s = jnp.dot(k, q, preferred_element_type=F32)
        s_refs[slot][:, lo:] = s
        mx_refs[slot][:, lo:] = jnp.max(s, axis=0, keepdims=True)

    def accumulate(blk, s, mx, lo=0):
        start = pl.multiple_of(blk * B_TK, B_TK)
        m_old = m_ref[:, lo:]
        m_new = jnp.maximum(m_old, mx)
        alpha = jnp.exp2(m_old - m_new)
        p = jnp.exp2(s - m_new).astype(BF16)
        vt = vt_ref[0, :, pl.ds(start, B_TK)]
        acc_ref[:, lo:] = alpha * acc_ref[:, lo:] + jnp.dot(vt, p, preferred_element_type=F32)
        m_ref[:, lo:] = m_new

    def diagonal_mask(s):
        kc = lax.broadcasted_iota(jnp.int32, s.shape, 0) // CHUNK
        qc = lax.broadcasted_iota(jnp.int32, s.shape, 1) // CHUNK
        return jnp.where(kc <= qc, s, NEG_INF)

    for u in range(B_AHEAD):
        scores(u, u, 0)

    def query_tile(qi, carry):
        q0 = qi * B_TQ
        m_ref[...] = jnp.full(m_ref.shape, NEG_INF, F32)
        acc_ref[...] = jnp.zeros(acc_ref.shape, F32)

        def steps(first_blk, n_blk):
            for u in range(n_blk):
                scores(first_blk + u + B_AHEAD, (u + B_AHEAD) % B_NB, q0)
                accumulate(first_blk + u, s_refs[u % B_NB][...], mx_refs[u % B_NB][...])

        def body(t, c):
            steps(t * 2 * B_NB, 2 * B_NB)
            return c

        lax.fori_loop(0, qi // 2, body, 0)

        @pl.when(qi % 2 == 1)
        def _():
            steps((qi - 1) * B_NB, B_NB)

        q0_next = jnp.minimum(qi + 1, n_q - 1) * B_TQ
        for u in range(B_NB):
            lo = u * B_TK
            if u + B_AHEAD < B_NB:
                scores(qi * B_NB + u + B_AHEAD, u + B_AHEAD, q0, lo=(u + B_AHEAD) * B_TK)
            else:
                scores(u + B_AHEAD - B_NB, u + B_AHEAD - B_NB, q0_next)
            s = diagonal_mask(s_refs[u][:, lo:])
            accumulate(qi * B_NB + u, s, jnp.max(s, axis=0, keepdims=True), lo=lo)
        denom = acc_ref[B_V_DIM:B_V_DIM + 1, :]
        o_ref[0, :, pl.ds(pl.multiple_of(q0, B_TQ), B_TQ)] = (acc_ref[0:B_V_DIM, :] * (1.0 / denom)).astype(BF16)
        return carry

    lax.fori_loop(0, n_q, query_tile, 0)


def _mixer_b(qbt, kb, vbt):
    B, S, _ = kb.shape
    return pl.pallas_call(
        _mixer_b_kernel,
        grid=(B, B_HEADS),
        in_specs=[pl.BlockSpec((1, LANES, S), lambda b, h: (b, h, 0)),
                  pl.BlockSpec((1, S, LANES), lambda b, h: (b, 0, h)),
                  pl.BlockSpec((1, VT_ROWS, S), lambda b, h: (b, h, 0))],
        out_specs=pl.BlockSpec((1, B_V_DIM, S), lambda b, h: (b, h, 0)),
        out_shape=jax.ShapeDtypeStruct((B, B_WIDTH, S), BF16),
        scratch_shapes=[pltpu.VMEM((B_TK, B_TQ), F32)] * B_NB + [pltpu.VMEM((1, B_TQ), F32)] * B_NB
                       + [pltpu.VMEM((1, B_TQ), F32), pltpu.VMEM((VT_ROWS, B_TQ), F32)],
        compiler_params=pltpu.CompilerParams(dimension_semantics=("arbitrary", "arbitrary"),
                                             vmem_limit_bytes=VMEM_LIMIT),
        name="mixer_b",
    )(qbt, kb, vbt)


def _merge_kernel(x_ref, yat_ref, ybt_ref, lng_ref, lnb_ref, wzg_ref, bzg_ref, wpa_ref, wpb_ref, wout_ref,
                  pg_ref, pb_ref, o_ref):
    rows = x_ref.shape[1] // MERGE_GROUPS
    for g in range(MERGE_GROUPS):
        r = slice(g * rows, (g + 1) * rows)
        h = _layer_norm(x_ref[0, r, :], lng_ref[...], lnb_ref[...])
        zg = jnp.dot(h.astype(BF16), wzg_ref[...], preferred_element_type=F32) + bzg_ref[...]
        za = zg[:, 0:A_WIDTH]
        zb = zg[:, A_WIDTH:A_WIDTH + B_WIDTH]
        ga = zg[:, A_WIDTH + B_WIDTH:A_WIDTH + B_WIDTH + D_MODEL]
        gb = zg[:, A_WIDTH + B_WIDTH + D_MODEL:]
        ya_in = yat_ref[0, :, r].astype(F32).T * (za * jax.nn.sigmoid(za))
        yb_in = ybt_ref[0, :, r].astype(F32).T * (zb * jax.nn.sigmoid(zb))
        ya = jnp.dot(ya_in.astype(BF16), wpa_ref[...], preferred_element_type=F32)
        yb = jnp.dot(yb_in.astype(BF16), wpb_ref[...], preferred_element_type=F32)
        mixed = jax.nn.sigmoid(ga) * ya + jax.nn.sigmoid(gb) * yb
        out = jnp.dot(mixed.astype(BF16), wout_ref[...], preferred_element_type=F32)
        o_ref[0, r, :] = _layer_norm(DEEPNORM_ALPHA * h + out, pg_ref[...], pb_ref[...])


def _merge(x, yat, ybt, lng, lnb, wzg, bzg, wpa, wpb, wout, pg, pb):
    B, S, _ = x.shape
    tm = OUT_TM
    row = lambda w: pl.BlockSpec((1, tm, w), lambda b, i: (b, i, 0))
    nzg = wzg.shape[1]
    return pl.pallas_call(
        _merge_kernel,
        grid=(B, S // tm),
        in_specs=[row(D_MODEL), pl.BlockSpec((1, A_WIDTH, tm), lambda b, i: (b, 0, i)),
                  pl.BlockSpec((1, B_WIDTH, tm), lambda b, i: (b, 0, i)),
                  _const_spec((1, D_MODEL)), _const_spec((1, D_MODEL)),
                  _const_spec((D_MODEL, nzg)), _const_spec((1, nzg)),
                  _const_spec((A_WIDTH, D_MODEL)), _const_spec((B_WIDTH, D_MODEL)),
                  _const_spec((D_MODEL, D_MODEL)), _const_spec((1, D_MODEL)), _const_spec((1, D_MODEL))],
        out_specs=row(D_MODEL),
        out_shape=jax.ShapeDtypeStruct((B, S, D_MODEL), F32),
        compiler_params=pltpu.CompilerParams(dimension_semantics=("arbitrary", "arbitrary"),
                                             vmem_limit_bytes=VMEM_LIMIT),
        name="merge_out",
    )(x, yat, ybt, lng, lnb, wzg, bzg, wpa, wpb, wout, pg, pb)


def _rot_cols(w):
    half = w.shape[-1] // 2
    return jnp.concatenate([-w[..., half:], w[..., :half]], axis=-1)


def _prep_layer(w_in, b_in, w_uq, w_ukv):
    c = 0
    cols = {}
    for name, width in (("aq", A_WIDTH), ("ak", A_WIDTH), ("av", A_WIDTH), ("az", A_WIDTH), ("cq", Q_LORA),
                        ("ckv", KV_LORA), ("kr", B_ROPE_DIM), ("bz", B_WIDTH), ("ga", D_MODEL), ("gb", D_MODEL)):
        cols[name] = slice(c, c + width)
        c += width
    w = lambda n: w_in[:, cols[n]]
    b = lambda n: b_in[cols[n]]
    a_scale = A_HEAD_DIM ** -0.5 * LOG2E
    w1 = jnp.concatenate([w("aq") * a_scale, w("ak"), w("cq"), w("ckv"),
                          w("kr"), w("kr"), _rot_cols(w("kr")), _rot_cols(w("kr"))], axis=1)
    b1 = jnp.concatenate([b("aq") * a_scale, b("ak"), b("cq"), b("ckv"),
                          b("kr"), b("kr"), _rot_cols(b("kr")), _rot_cols(b("kr"))])
    wzg = jnp.concatenate([w("az"), w("bz"), w("ga"), w("gb")], axis=1)
    bzg = jnp.concatenate([b("az"), b("bz"), b("ga"), b("gb")])

    uq = w_uq.reshape(Q_LORA, B_HEADS, B_QK_DIM)
    uq_rope = uq[:, :, B_NOPE_DIM:]
    wq = jnp.concatenate([uq, _rot_cols(uq_rope)], axis=-1).reshape(Q_LORA, B_HEADS * LANES)
    ukv = w_ukv.reshape(KV_LORA, B_HEADS, B_NOPE_DIM + B_V_DIM)
    wk = jnp.concatenate([ukv[:, :, :B_NOPE_DIM], jnp.zeros((KV_LORA, B_HEADS, LANES - B_NOPE_DIM), F32)],
                         axis=-1).reshape(KV_LORA, B_HEADS * LANES)
    wvt = ukv[:, :, B_NOPE_DIM:].reshape(KV_LORA, B_WIDTH).T
    return (w1.astype(BF16), b1[None, :], w("av").T.astype(BF16), b("av")[:, None], wq.T.astype(BF16),
            wk.astype(BF16), wvt.astype(BF16), wzg.astype(BF16), bzg[None, :])


def _rope_freq_col():
    half = B_ROPE_DIM // 2
    inv_freq = ROPE_THETA ** (-jnp.arange(half, dtype=F32) / half)
    return inv_freq[:, None]


def kernel(x, positions, ln_in_g, ln_in_b, w_in, b_in, q_norm_g, kv_norm_g, w_uq, w_ukv, rel_bias, w_proj_a,
           w_proj_b, w_out, ln_post_g, ln_post_b):
    depth = w_in.shape[0]
    assert depth == 1, "the trunk-entry norm is recomputed per kernel, which is only valid for one layer"
    B, S, _ = x.shape
    pos3 = positions.reshape(B, 1, S)
    lng, lnb = ln_in_g[None, :], ln_in_b[None, :]
    freq = _rope_freq_col()
    l = 0
    w1, b1, wavt, bav, wqt, wk, wvt, wzg, bzg = _prep_layer(w_in[l], b_in[l], w_uq[l], w_ukv[l])
    aq, ak, avt, qbt, kb, vbt = _token_projections(x, pos3, lng, lnb, w1, b1, wavt, bav, q_norm_g[l][None, :],
                                                   kv_norm_g[l][None, :], wqt, wk, wvt, freq)
    yat = _mixer_a(aq, ak, avt, _mixer_a_bias(rel_bias[l]))
    ybt = _mixer_b(qbt, kb, vbt)
    return _merge(x, yat, ybt, lng, lnb, wzg, bzg, w_proj_a[l].astype(BF16), w_proj_b[l].astype(BF16),
                  w_out[l].astype(BF16), ln_post_g[l][None, :], ln_post_b[l][None, :])
```

```python
import jax
import jax.numpy as jnp
from jax import lax
from jax.experimental import pallas as pl
from jax.experimental.pallas import tpu as pltpu

D_MODEL = 1024
CHUNK = 64
A_HEADS = 8
A_HEAD_DIM = 64
A_WIDTH = A_HEADS * A_HEAD_DIM
A_LEFT_CHUNKS = 8
REL_CLIP = 128
B_HEADS = 8
B_NOPE_DIM = 64
B_ROPE_DIM = 32
B_QK_DIM = B_NOPE_DIM + B_ROPE_DIM
B_V_DIM = 64
B_WIDTH = B_HEADS * B_V_DIM
Q_LORA = 256
KV_LORA = 128
ROPE_THETA = 10000.0
DEEPNORM_ALPHA = 2.0 ** 0.25
LN_EPS = 1e-5
RMS_EPS = 1e-6
NEG_INF = -1e30

LANES = 128
VMEM_LIMIT = 56 * 1024 * 1024

PROJ_TM = 1024
A_TQ = 256
A_WIN = A_TQ + A_LEFT_CHUNKS * CHUNK
A_SUB = 4
A_STEP = A_SUB * A_TQ
A_ROLL = 1024
B_TQ = 1024
B_TK = 256
B_NB = B_TQ // B_TK
B_AHEAD = 2
V_DIM = 64
VT_ROWS = 80
LOG2E = 1.4426950408889634
OUT_TM = 512
MERGE_GROUPS = 2

BF16 = jnp.bfloat16
F32 = jnp.float32


def _layer_norm(x, g, b):
    mu = jnp.mean(x, axis=-1, keepdims=True)
    xc = x - mu
    var = jnp.mean(xc * xc, axis=-1, keepdims=True)
    return xc * lax.rsqrt(var + LN_EPS) * g + b


def _rms_norm(x, g):
    return x * lax.rsqrt(jnp.mean(x * x, axis=-1, keepdims=True) + RMS_EPS) * g


def _const_spec(shape):
    nd = len(shape)
    return pl.BlockSpec(shape, lambda *_: (0,) * nd, pipeline_mode=pl.Buffered(1))


def _store_values_t(ref, vt):
    tail = VT_ROWS - V_DIM
    ones_rows = (lax.broadcasted_iota(jnp.int32, (tail, vt.shape[1]), 0) == 0).astype(BF16)
    for hd in range(vt.shape[0] // V_DIM):
        ref[0, hd * VT_ROWS:hd * VT_ROWS + V_DIM, :] = vt[hd * V_DIM:(hd + 1) * V_DIM].astype(BF16)
        ref[0, hd * VT_ROWS + V_DIM:(hd + 1) * VT_ROWS, :] = ones_rows


def _proj_kernel(x_ref, pos_ref, lng_ref, lnb_ref, w1_ref, b1_ref, wavt_ref, bav_ref, qg_ref, kvg_ref, wqt_ref,
                 wk_ref, wvt_ref, freq_ref, aq_ref, ak_ref, avt_ref, qbt_ref, kb_ref, vbt_ref):
    hb = _layer_norm(x_ref[0], lng_ref[...], lnb_ref[...]).astype(BF16)
    proj = jnp.dot(hb, w1_ref[...], preferred_element_type=F32) + b1_ref[...]
    aq_ref[0] = proj[:, 0:A_WIDTH].astype(BF16)
    ak_ref[0] = proj[:, A_WIDTH:2 * A_WIDTH].astype(BF16)
    avt = lax.dot_general(wavt_ref[...], hb, (((1,), (1,)), ((), ())), preferred_element_type=F32) + bav_ref[...]
    _store_values_t(avt_ref, avt)
    o = 2 * A_WIDTH
    cq = _rms_norm(proj[:, o:o + Q_LORA], qg_ref[...]).astype(BF16)
    o += Q_LORA
    ckv = _rms_norm(proj[:, o:o + KV_LORA], kvg_ref[...]).astype(BF16)
    o += KV_LORA
    kr = proj[:, o:o + LANES]

    ang_t = freq_ref[...] * pos_ref[0].astype(F32)
    cos16 = jnp.cos(ang_t)
    sin16 = jnp.sin(ang_t)
    scale = B_QK_DIM ** -0.5 * LOG2E
    tq_t = jnp.concatenate([jnp.full((B_NOPE_DIM, ang_t.shape[1]), scale, F32),
                            cos16 * scale, cos16 * scale, sin16 * scale, sin16 * scale], axis=0)
    lane = lax.broadcasted_iota(jnp.int32, (1, LANES), 1)
    tk = jnp.concatenate([cos16] * 4 + [sin16] * 4, axis=0).T
    prod = kr * tk
    k_rope = jnp.where(lane >= B_NOPE_DIM, prod + pltpu.roll(prod, LANES // 2, 1), 0.0)

    q_all_t = lax.dot_general(wqt_ref[...], cq, (((1,), (1,)), ((), ())), preferred_element_type=F32)
    k_all = jnp.dot(ckv, wk_ref[...], preferred_element_type=F32)
    for hd in range(B_HEADS):
        sl = slice(hd * LANES, (hd + 1) * LANES)
        qbt_ref[0, sl, :] = (q_all_t[sl, :] * tq_t).astype(BF16)
        kb_ref[0, :, sl] = (k_all[:, sl] + k_rope).astype(BF16)
    vt = lax.dot_general(wvt_ref[...], ckv, (((1,), (1,)), ((), ())), preferred_element_type=F32)
    _store_values_t(vbt_ref, vt)


def _token_projections(x, pos3, lng, lnb, w1, b1, wavt, bav, qg, kvg, wqt, wk, wvt, freq):
    B, S, _ = x.shape
    tm = PROJ_TM
    n1 = w1.shape[1]
    row = lambda w: pl.BlockSpec((1, tm, w), lambda b, i: (b, i, 0))
    vt_spec = pl.BlockSpec((1, B_HEADS * VT_ROWS, tm), lambda b, i: (b, 0, i))
    vt_shape = jax.ShapeDtypeStruct((B, B_HEADS * VT_ROWS, S), BF16)
    return pl.pallas_call(
        _proj_kernel,
        grid=(B, S // tm),
        in_specs=[row(D_MODEL), pl.BlockSpec((1, 1, tm), lambda b, i: (b, 0, i)),
                  _const_spec((1, D_MODEL)), _const_spec((1, D_MODEL)),
                  _const_spec((D_MODEL, n1)), _const_spec((1, n1)),
                  _const_spec((A_WIDTH, D_MODEL)), _const_spec((A_WIDTH, 1)),
                  _const_spec((1, Q_LORA)), _const_spec((1, KV_LORA)),
                  _const_spec((B_HEADS * LANES, Q_LORA)), _const_spec((KV_LORA, B_HEADS * LANES)),
                  _const_spec((B_WIDTH, KV_LORA)), _const_spec((B_ROPE_DIM // 2, 1))],
        out_specs=[row(A_WIDTH), row(A_WIDTH), vt_spec,
                   pl.BlockSpec((1, B_HEADS * LANES, tm), lambda b, i: (b, 0, i)), row(B_HEADS * LANES), vt_spec],
        out_shape=[jax.ShapeDtypeStruct((B, S, A_WIDTH), BF16)] * 2 + [vt_shape]
                  + [jax.ShapeDtypeStruct((B, B_HEADS * LANES, S), BF16),
                     jax.ShapeDtypeStruct((B, S, B_HEADS * LANES), BF16), vt_shape],
        compiler_params=pltpu.CompilerParams(dimension_semantics=("arbitrary", "arbitrary"),
                                             vmem_limit_bytes=VMEM_LIMIT),
        name="token_projections",
    )(x, pos3, lng, lnb, w1, b1, wavt, bav, qg, kvg, wqt, wk, wvt, freq)


def _mixer_a_kernel(q_ref, k1_ref, k0_ref, v1_ref, v0_ref, bias_ref, o_ref, s0_ref, s1_ref, mx0_ref, mx1_ref):
    i = pl.program_id(1)
    s_refs, mx_refs = (s0_ref, s1_ref), (mx0_ref, mx1_ref)
    n_old = A_LEFT_CHUNKS * CHUNK
    lane = lax.broadcasted_iota(jnp.int32, (1, LANES), 1)
    row = lax.broadcasted_iota(jnp.int32, (n_old, 1), 0)

    def scores(unit, slot):
        t, pair = divmod(unit, A_HEADS // 2)
        sl = slice(pair * LANES, (pair + 1) * LANES)
        qp = q_ref[0, t * A_TQ:(t + 1) * A_TQ, sl]
        kwin = jnp.concatenate([k1_ref[0, A_STEP - n_old:, sl], k0_ref[0, :, sl]], axis=0)
        kp = kwin[t * A_TQ:t * A_TQ + A_WIN]
        q2 = jnp.concatenate([jnp.where((lane // A_HEAD_DIM) == hh, qp, jnp.zeros_like(qp)) for hh in range(2)],
                             axis=0)
        s = lax.dot_general(kp, q2, (((1,), (1,)), ((), ())), preferred_element_type=F32)
        s = s + jnp.concatenate([bias_ref[2 * pair], bias_ref[2 * pair + 1]], axis=1)
        kpos = row + (i * A_STEP + t * A_TQ - n_old)
        pad_mask = jnp.where(kpos >= 0, 0.0, NEG_INF).astype(F32)
        s = jnp.concatenate([s[:n_old] + pad_mask, s[n_old:]], axis=0)
        s_refs[slot][...] = s
        mx_refs[slot][...] = jnp.max(s, axis=0, keepdims=True)

    def finish(unit, slot):
        t, pair = divmod(unit, A_HEADS // 2)
        p = jnp.exp2(s_refs[slot][...] - mx_refs[slot][...]).astype(BF16)
        rows = slice(2 * pair * VT_ROWS, (2 * pair + 2) * VT_ROWS)
        vwin = jnp.concatenate([v1_ref[0, rows, A_STEP - n_old:], v0_ref[0, rows, :]], axis=1)
        vt = vwin[:, t * A_TQ:t * A_TQ + A_WIN]
        for hh in range(2):
            oh = jnp.dot(vt[hh * VT_ROWS:(hh + 1) * VT_ROWS], p[:, hh * A_TQ:(hh + 1) * A_TQ],
                         preferred_element_type=F32)
            hd = 2 * pair + hh
            o_ref[0, hd * V_DIM:(hd + 1) * V_DIM, t * A_TQ:(t + 1) * A_TQ] = (
                oh[:V_DIM] * (1.0 / oh[V_DIM:V_DIM + 1])).astype(BF16)

    n_units = A_SUB * A_HEADS // 2
    scores(0, 0)
    for unit in range(n_units):
        if unit + 1 < n_units:
            scores(unit + 1, (unit + 1) % 2)
        finish(unit, unit % 2)


def _mixer_a(aq, ak, avt, bias):
    B, S, W = aq.shape
    blk = lambda back: pl.BlockSpec((1, A_STEP, W), lambda b, i: (b, jnp.maximum(i - back, 0), 0))
    vblk = lambda back: pl.BlockSpec((1, A_HEADS * VT_ROWS, A_STEP), lambda b, i: (b, 0, jnp.maximum(i - back, 0)))
    return pl.pallas_call(
        _mixer_a_kernel,
        grid=(B, S // A_STEP),
        in_specs=[blk(0), blk(1), blk(0), vblk(1), vblk(0), _const_spec(bias.shape)],
        out_specs=pl.BlockSpec((1, W, A_STEP), lambda b, i: (b, 0, i)),
        out_shape=jax.ShapeDtypeStruct((B, W, S), BF16),
        scratch_shapes=[pltpu.VMEM((A_WIN, 2 * A_TQ), F32)] * 2 + [pltpu.VMEM((1, 2 * A_TQ), F32)] * 2,
        compiler_params=pltpu.CompilerParams(dimension_semantics=("arbitrary", "arbitrary"),
                                             vmem_limit_bytes=VMEM_LIMIT),
        name="mixer_a",
    )(aq, ak, ak, avt, avt, bias)


def _bias_kernel(row_ref, o_ref):
    rows = jnp.broadcast_to(row_ref[0], (A_WIN, A_ROLL))
    table = pltpu.roll(rows, 0, 1, stride=1, stride_axis=0)[:, :A_TQ]
    kc = lax.broadcasted_iota(jnp.int32, (A_WIN, A_TQ), 0) // CHUNK
    qc = lax.broadcasted_iota(jnp.int32, (A_WIN, A_TQ), 1) // CHUNK
    gap = qc + A_LEFT_CHUNKS - kc
    o_ref[0] = jnp.where((gap >= 0) & (gap <= A_LEFT_CHUNKS), table * LOG2E, NEG_INF)


def _mixer_a_bias(rel_bias):
    tbl = rel_bias.T.astype(F32)
    n_old = A_LEFT_CHUNKS * CHUNK
    first_tbl = A_ROLL - n_old - REL_CLIP
    far = jnp.broadcast_to(tbl[:, -1:], (A_HEADS, A_ROLL))
    near = jnp.broadcast_to(tbl[:, :1], (A_HEADS, A_ROLL))
    row = jnp.concatenate([far[:, :A_ROLL - A_WIN + 1], near[:, A_ROLL - A_WIN + 1:first_tbl], tbl,
                           far[:, first_tbl + tbl.shape[1]:]], axis=1)
    row = row[:, None, :]
    return pl.pallas_call(
        _bias_kernel,
        grid=(A_HEADS,),
        in_specs=[pl.BlockSpec((1, 1, A_ROLL), lambda h: (h, 0, 0))],
        out_specs=pl.BlockSpec((1, A_WIN, A_TQ), lambda h: (h, 0, 0)),
        out_shape=jax.ShapeDtypeStruct((A_HEADS, A_WIN, A_TQ), F32),
        name="mixer_a_bias",
    )(row)


def _mixer_b_kernel(q_ref, k_ref, vt_ref, o_ref, *scratch):
    s_refs = scratch[0:B_NB]
    mx_refs = scratch[B_NB:2 * B_NB]
    m_ref, acc_ref = scratch[2 * B_NB:]
    n_q = q_ref.shape[2] // B_TQ

    def scores(blk, slot, q0, lo=0):
        start = pl.multiple_of(blk * B_TK, B_TK)
        k = k_ref[0, pl.ds(start, B_TK), :]
        q = q_ref[0, :, pl.ds(pl.multiple_of(q0 + lo, B_TK), B_TQ - lo)]
        s = jnp.dot(k, q, preferred_element_type=F32)
        s_refs[slot][:, lo:] = s
        mx_refs[slot][:, lo:] = jnp.max(s, axis=0, keepdims=True)

    def accumulate(blk, s, mx, lo=0):
        start = pl.multiple_of(blk * B_TK, B_TK)
        m_old = m_ref[:, lo:]
        m_new = jnp.maximum(m_old, mx)
        alpha = jnp.exp2(m_old - m_new)
        p = jnp.exp2(s - m_new).astype(BF16)
        vt = vt_ref[0, :, pl.ds(start, B_TK)]
        acc_ref[:, lo:] = alpha * acc_ref[:, lo:] + jnp.dot(vt, p, preferred_element_type=F32)
        m_ref[:, lo:] = m_new

    def diagonal_mask(s):
        kc = lax.broadcasted_iota(jnp.int32, s.shape, 0) // CHUNK
        qc = lax.broadcasted_iota(jnp.int32, s.shape, 1) // CHUNK
        return jnp.where(kc <= qc, s, NEG_INF)

    for u in range(B_AHEAD):
        scores(u, u, 0)

    def query_tile(qi, carry):
        q0 = qi * B_TQ
        m_ref[...] = jnp.full(m_ref.shape, NEG_INF, F32)
        acc_ref[...] = jnp.zeros(acc_ref.shape, F32)

        def steps(first_blk, n_blk):
            for u in range(n_blk):
                scores(first_blk + u + B_AHEAD, (u + B_AHEAD) % B_NB, q0)
                accumulate(first_blk + u, s_refs[u % B_NB][...], mx_refs[u % B_NB][...])

        def body(t, c):
            steps(t * 2 * B_NB, 2 * B_NB)
            return c

        lax.fori_loop(0, qi // 2, body, 0)

        @pl.when(qi % 2 == 1)
        def _():
            steps((qi - 1) * B_NB, B_NB)

        q0_next = jnp.minimum(qi + 1, n_q - 1) * B_TQ
        for u in range(B_NB):
            lo = u * B_TK
            if u + B_AHEAD < B_NB:
                scores(qi * B_NB + u + B_AHEAD, u + B_AHEAD, q0, lo=(u + B_AHEAD) * B_TK)
            else:
                scores(u + B_AHEAD - B_NB, u + B_AHEAD - B_NB, q0_next)
            s = diagonal_mask(s_refs[u][:, lo:])
            accumulate(qi * B_NB + u, s, jnp.max(s, axis=0, keepdims=True), lo=lo)
        denom = acc_ref[B_V_DIM:B_V_DIM + 1, :]
        o_ref[0, :, pl.ds(pl.multiple_of(q0, B_TQ), B_TQ)] = (acc_ref[0:B_V_DIM, :] * (1.0 / denom)).astype(BF16)
        return carry

    lax.fori_loop(0, n_q, query_tile, 0)


def _mixer_b(qbt, kb, vbt):
    B, S, _ = kb.shape
    return pl.pallas_call(
        _mixer_b_kernel,
        grid=(B, B_HEADS),
        in_specs=[pl.BlockSpec((1, LANES, S), lambda b, h: (b, h, 0)),
                  pl.BlockSpec((1, S, LANES), lambda b, h: (b, 0, h)),
                  pl.BlockSpec((1, VT_ROWS, S), lambda b, h: (b, h, 0))],
        out_specs=pl.BlockSpec((1, B_V_DIM, S), lambda b, h: (b, h, 0)),
        out_shape=jax.ShapeDtypeStruct((B, B_WIDTH, S), BF16),
        scratch_shapes=[pltpu.VMEM((B_TK, B_TQ), F32)] * B_NB + [pltpu.VMEM((1, B_TQ), F32)] * B_NB
                       + [pltpu.VMEM((1, B_TQ), F32), pltpu.VMEM((VT_ROWS, B_TQ), F32)],
        compiler_params=pltpu.CompilerParams(dimension_semantics=("arbitrary", "arbitrary"),
                                             vmem_limit_bytes=VMEM_LIMIT),
        name="mixer_b",
    )(qbt, kb, vbt)


def _merge_kernel(x_ref, yat_ref, ybt_ref, lng_ref, lnb_ref, wzg_ref, bzg_ref, wpa_ref, wpb_ref, wout_ref,
                  pg_ref, pb_ref, o_ref):
    rows = x_ref.shape[1] // MERGE_GROUPS
    for g in range(MERGE_GROUPS):
        r = slice(g * rows, (g + 1) * rows)
        h = _layer_norm(x_ref[0, r, :], lng_ref[...], lnb_ref[...])
        zg = jnp.dot(h.astype(BF16), wzg_ref[...], preferred_element_type=F32) + bzg_ref[...]
        za = zg[:, 0:A_WIDTH]
        zb = zg[:, A_WIDTH:A_WIDTH + B_WIDTH]
        ga = zg[:, A_WIDTH + B_WIDTH:A_WIDTH + B_WIDTH + D_MODEL]
        gb = zg[:, A_WIDTH + B_WIDTH + D_MODEL:]
        ya_in = yat_ref[0, :, r].astype(F32).T * (za * jax.nn.sigmoid(za))
        yb_in = ybt_ref[0, :, r].astype(F32).T * (zb * jax.nn.sigmoid(zb))
        ya = jnp.dot(ya_in.astype(BF16), wpa_ref[...], preferred_element_type=F32)
        yb = jnp.dot(yb_in.astype(BF16), wpb_ref[...], preferred_element_type=F32)
        mixed = jax.nn.sigmoid(ga) * ya + jax.nn.sigmoid(gb) * yb
        out = jnp.dot(mixed.astype(BF16), wout_ref[...], preferred_element_type=F32)
        o_ref[0, r, :] = _layer_norm(DEEPNORM_ALPHA * h + out, pg_ref[...], pb_ref[...])


def _merge(x, yat, ybt, lng, lnb, wzg, bzg, wpa, wpb, wout, pg, pb):
    B, S, _ = x.shape
    tm = OUT_TM
    row = lambda w: pl.BlockSpec((1, tm, w), lambda b, i: (b, i, 0))
    nzg = wzg.shape[1]
    return pl.pallas_call(
        _merge_kernel,
        grid=(B, S // tm),
        in_specs=[row(D_MODEL), pl.BlockSpec((1, A_WIDTH, tm), lambda b, i: (b, 0, i)),
                  pl.BlockSpec((1, B_WIDTH, tm), lambda b, i: (b, 0, i)),
                  _const_spec((1, D_MODEL)), _const_spec((1, D_MODEL)),
                  _const_spec((D_MODEL, nzg)), _const_spec((1, nzg)),
                  _const_spec((A_WIDTH, D_MODEL)), _const_spec((B_WIDTH, D_MODEL)),
                  _const_spec((D_MODEL, D_MODEL)), _const_spec((1, D_MODEL)), _const_spec((1, D_MODEL))],
        out_specs=row(D_MODEL),
        out_shape=jax.ShapeDtypeStruct((B, S, D_MODEL), F32),
        compiler_params=pltpu.CompilerParams(dimension_semantics=("arbitrary", "arbitrary"),
                                             vmem_limit_bytes=VMEM_LIMIT),
        name="merge_out",
    )(x, yat, ybt, lng, lnb, wzg, bzg, wpa, wpb, wout, pg, pb)


def _rot_cols(w):
    half = w.shape[-1] // 2
    return jnp.concatenate([-w[..., half:], w[..., :half]], axis=-1)


def _prep_layer(w_in, b_in, w_uq, w_ukv):
    c = 0
    cols = {}
    for name, width in (("aq", A_WIDTH), ("ak", A_WIDTH), ("av", A_WIDTH), ("az", A_WIDTH), ("cq", Q_LORA),
                        ("ckv", KV_LORA), ("kr", B_ROPE_DIM), ("bz", B_WIDTH), ("ga", D_MODEL), ("gb", D_MODEL)):
        cols[name] = slice(c, c + width)
        c += width
    w = lambda n: w_in[:, cols[n]]
    b = lambda n: b_in[cols[n]]
    a_scale = A_HEAD_DIM ** -0.5 * LOG2E
    w1 = jnp.concatenate([w("aq") * a_scale, w("ak"), w("cq"), w("ckv"),
                          w("kr"), w("kr"), _rot_cols(w("kr")), _rot_cols(w("kr"))], axis=1)
    b1 = jnp.concatenate([b("aq") * a_scale, b("ak"), b("cq"), b("ckv"),
                          b("kr"), b("kr"), _rot_cols(b("kr")), _rot_cols(b("kr"))])
    wzg = jnp.concatenate([w("az"), w("bz"), w("ga"), w("gb")], axis=1)
    bzg = jnp.concatenate([b("az"), b("bz"), b("ga"), b("gb")])

    uq = w_uq.reshape(Q_LORA, B_HEADS, B_QK_DIM)
    uq_rope = uq[:, :, B_NOPE_DIM:]
    wq = jnp.concatenate([uq, _rot_cols(uq_rope)], axis=-1).reshape(Q_LORA, B_HEADS * LANES)
    ukv = w_ukv.reshape(KV_LORA, B_HEADS, B_NOPE_DIM + B_V_DIM)
    wk = jnp.concatenate([ukv[:, :, :B_NOPE_DIM], jnp.zeros((KV_LORA, B_HEADS, LANES - B_NOPE_DIM), F32)],
                         axis=-1).reshape(KV_LORA, B_HEADS * LANES)
    wvt = ukv[:, :, B_NOPE_DIM:].reshape(KV_LORA, B_WIDTH).T
    return (w1.astype(BF16), b1[None, :], w("av").T.astype(BF16), b("av")[:, None], wq.T.astype(BF16),
            wk.astype(BF16), wvt.astype(BF16), wzg.astype(BF16), bzg[None, :])


def _rope_freq_col():
    half = B_ROPE_DIM // 2
    inv_freq = ROPE_THETA ** (-jnp.arange(half, dtype=F32) / half)
    return inv_freq[:, None]


def kernel(x, positions, ln_in_g, ln_in_b, w_in, b_in, q_norm_g, kv_norm_g, w_uq, w_ukv, rel_bias, w_proj_a,
           w_proj_b, w_out, ln_post_g, ln_post_b):
    depth = w_in.shape[0]
    assert depth == 1, "the trunk-entry norm is recomputed per kernel, which is only valid for one layer"
    B, S, _ = x.shape
    pos3 = positions.reshape(B, 1, S)
    lng, lnb = ln_in_g[None, :], ln_in_b[None, :]
    freq = _rope_freq_col()
    l = 0
    w1, b1, wavt, bav, wqt, wk, wvt, wzg, bzg = _prep_layer(w_in[l], b_in[l], w_uq[l], w_ukv[l])
    aq, ak, avt, qbt, kb, vbt = _token_projections(x, pos3, lng, lnb, w1, b1, wavt, bav, q_norm_g[l][None, :],
                                                   kv_norm_g[l][None, :], wqt, wk, wvt, freq)
    yat = _mixer_a(aq, ak, avt, _mixer_a_bias(rel_bias[l]))
    ybt = _mixer_b(qbt, kb, vbt)
    return _merge(x, yat, ybt, lng, lnb, wzg, bzg, w_proj_a[l].astype(BF16), w_proj_b[l].astype(BF16),
                  w_out[l].astype(BF16), ln_post_g[l][None, :], ln_post_b[l][None, :])
```
